```python
import numpy as np
import jax
import jax.numpy as jnp
from jax import lax

D_MODEL = 1024
BATCH = 2
SEQ = 8192
DEPTH = 4

N_A_LAYERS = DEPTH // 2
N_B_LAYERS = DEPTH - N_A_LAYERS
ALPHA = (2.0 * DEPTH) ** 0.25
BETA = (8.0 * DEPTH) ** -0.25
LN_EPS = 1e-5
NEG_INF = -1e30

RET_HEADS = 4
RET_QK_DIM = D_MODEL // RET_HEADS
RET_V_DIM = 2 * D_MODEL // RET_HEADS
RET_CHUNK = 128
RET_IN = 2 * RET_HEADS * RET_QK_DIM + 2 * RET_HEADS * RET_V_DIM

NSA_HEADS = 16
NSA_GROUPS = 4
NSA_REP = NSA_HEADS // NSA_GROUPS
NSA_HEAD_DIM = D_MODEL // NSA_HEADS
N_BRANCH = 3
CMP_STRIDE = 16
CMP_LEN = 2 * CMP_STRIDE
CMP_HIDDEN = 256
SLC_BLOCK = 64
SLC_TOPK = 16
WINDOW = 512
Q_BLOCK = 128
FORCE_BONUS = 100.0
NSA_IN = NSA_HEADS * NSA_HEAD_DIM + NSA_HEADS * N_BRANCH
NSA_KV = N_BRANCH * 2 * NSA_GROUPS * NSA_HEAD_DIM

PEER_HEADS = 8
PEER_NKEYS = 128
PEER_EXPERTS = PEER_NKEYS * PEER_NKEYS
PEER_TOPK = 16
PEER_QDIM = 256
PEER_TOKENS = 128

kernel_name = "yoco_retention_nsa_peer_deepnorm_adaln"


def layer_norm(x, g, b):
    xf = x.astype(jnp.float32)
    mu = jnp.mean(xf, axis=-1, keepdims=True)
    var = jnp.mean(jnp.square(xf - mu), axis=-1, keepdims=True)
    return ((xf - mu) * lax.rsqrt(var + LN_EPS) * g + b).astype(x.dtype)


def masked_softmax(s, mask):
    s = jnp.where(mask, s, NEG_INF)
    m = jnp.max(s, axis=-1, keepdims=True)
    e = jnp.where(mask, jnp.exp(s - m), 0.0)
    return e / jnp.maximum(jnp.sum(e, axis=-1, keepdims=True), 1e-30)


def rotate(x, cos, sin):
    x1, x2 = jnp.split(x, 2, axis=-1)
    return jnp.concatenate([x1 * cos - x2 * sin, x1 * sin + x2 * cos], axis=-1)


def retention(h, w_in, w_o):
    B, S, _ = h.shape
    H, dk, dv, C = RET_HEADS, RET_QK_DIM, RET_V_DIM, RET_CHUNK
    f32 = jnp.float32
    proj = h @ w_in
    q, k, v, g = jnp.split(proj, [H * dk, 2 * H * dk, 2 * H * dk + H * dv], axis=-1)
    to_heads = lambda t, d: t.reshape(B, S, H, d).transpose(0, 2, 1, 3).astype(f32)
    q, k, v = to_heads(q, dk), to_heads(k, dk), to_heads(v, dv)
    pos = jnp.arange(S, dtype=f32)
    theta = 1.0 / (10000.0 ** jnp.linspace(0.0, 1.0, dk // 2, dtype=f32))
    ang = pos[:, None] * theta[None, :]
    cos, sin = jnp.cos(ang), jnp.sin(ang)
    q = rotate(q, cos, sin)
    k = rotate(k, cos, sin) * (dk ** -0.5)
    log_g = jnp.log1p(-jnp.exp2(-5.0 - jnp.arange(H, dtype=f32)))
    idx = jnp.arange(C, dtype=f32)
    diff = idx[:, None] - idx[None, :]
    decay = jnp.where(diff >= 0, jnp.exp(jnp.maximum(diff, 0.0)[None] * log_g[:, None, None]), 0.0)
    q_dec = jnp.exp((idx + 1.0)[None] * log_g[:, None])
    k_dec = jnp.exp((C - 1.0 - idx)[None] * log_g[:, None])
    c_dec = jnp.exp(C * log_g)
    n_chunks = S // C
    chunks = lambda t: t.reshape(B, H, n_chunks, C, t.shape[-1]).transpose(2, 0, 1, 3, 4)

    def step(state, qkv):
        qc, kc, vc = qkv
        inner = jnp.einsum('bhqk,bhke->bhqe', jnp.einsum('bhqd,bhkd->bhqk', qc, kc) * decay, vc)
        cross = jnp.einsum('bhqd,bhde->bhqe', qc, state) * q_dec[:, :, None]
        state = state * c_dec[:, None, None] + jnp.einsum('bhkd,bhke->bhde', kc * k_dec[:, :, None], vc)
        return state, inner + cross

    state0 = jnp.zeros((B, H, dk, dv), f32)
    _, o = lax.scan(step, state0, (chunks(q), chunks(k), chunks(v)))
    o = o.transpose(1, 0, 3, 2, 4).reshape(B, S, H, dv)
    mu = jnp.mean(o, axis=-1, keepdims=True)
    var = jnp.mean(jnp.square(o - mu), axis=-1, keepdims=True)
    o = ((o - mu) * lax.rsqrt(var + LN_EPS)).reshape(B, S, H * dv)
    return (jax.nn.silu(g.astype(f32)) * o).astype(h.dtype) @ w_o


def nsa_shared_kv(xs, w_kv, cmp_pe, cmp_w1, cmp_b1, cmp_w2):
    B, S, _ = xs.shape
    G, hd = NSA_GROUPS, NSA_HEAD_DIM
    kv = (xs @ w_kv).reshape(B, S, N_BRANCH, 2, G, hd).transpose(2, 3, 0, 4, 1, 5)
    pieces = kv[0].reshape(2, B, G, S // CMP_STRIDE, CMP_STRIDE, hd)
    blocks = jnp.concatenate([pieces[:, :, :, :-1], pieces[:, :, :, 1:]], axis=4)
    blocks = blocks + cmp_pe[:, None, None, None]
    n_cmp = blocks.shape[3]
    flat = blocks.reshape(2, B, G, n_cmp, CMP_LEN * hd)
    hid = jax.nn.gelu(jnp.einsum('cbgnf,cfh->cbgnh', flat, cmp_w1) + cmp_b1[:, None, None, None])
    comp = jnp.einsum('cbgnh,chd->cbgnd', hid, cmp_w2)
    slc = kv[1].reshape(2, B, G, S // SLC_BLOCK, SLC_BLOCK, hd)
    win = jnp.pad(kv[2], ((0, 0), (0, 0), (0, 0), (WINDOW, 0), (0, 0)))
    return comp[0], comp[1], slc[0], slc[1], win[0], win[1]


def cmp_to_slc_matrix(n_cmp, n_slc):
    i = np.arange(n_cmp)[:, None] * CMP_STRIDE
    j = np.arange(n_slc)[None, :] * SLC_BLOCK
    ov = np.minimum(i + CMP_LEN, j + SLC_BLOCK) - np.maximum(i, j)
    return jnp.asarray(np.clip(ov, 0, None) / CMP_LEN, dtype=jnp.float32)


def nsa(h, w_in, w_o, k_cmp, v_cmp, k_slc, v_slc, k_win, v_win):
    B, S, _ = h.shape
    G, R, hd, QB = NSA_GROUPS, NSA_REP, NSA_HEAD_DIM, Q_BLOCK
    f32 = jnp.float32
    n_cmp = k_cmp.shape[2]
    n_slc = k_slc.shape[2]
    n_sel = min(SLC_TOPK, n_slc)
    proj = h @ w_in
    q = proj[..., :NSA_HEADS * hd].reshape(B, S, G, R, hd) * (hd ** -0.5)
    gate = jax.nn.sigmoid(proj[..., NSA_HEADS * hd:].astype(f32)).reshape(B, S, G, R, N_BRANCH)
    nqb = S // QB
    q_blocks = q.reshape(B, nqb, QB, G, R, hd).transpose(1, 0, 3, 4, 2, 5)
    g_blocks = gate.reshape(B, nqb, QB, G, R, N_BRANCH).transpose(1, 0, 3, 4, 2, 5)
    starts = jnp.arange(nqb, dtype=jnp.int32) * QB
    c2s = cmp_to_slc_matrix(n_cmp, n_slc)
    cmp_end = jnp.arange(n_cmp) * CMP_STRIDE + CMP_LEN - 1
    slc_idx = jnp.arange(n_slc)
    slc_start = slc_idx * SLC_BLOCK
    bi = jnp.arange(B)[:, None, None, None]
    gi = jnp.arange(G)[None, :, None, None]

    def block(args):
        qb, gb, start = args
        t = start + jnp.arange(QB)
        s = jnp.einsum('bgrqd,bgnd->bgrqn', qb, k_cmp, preferred_element_type=f32)
        p_cmp = masked_softmax(s, cmp_end[None, :] <= t[:, None])
        o_cmp = jnp.einsum('bgrqn,bgnd->bgrqd', p_cmp.astype(v_cmp.dtype), v_cmp)
        imp = jnp.einsum('bgqn,nj->bgqj', p_cmp.sum(axis=2), c2s)
        cur = t // SLC_BLOCK
        forced = (slc_idx[None] == 0) | (slc_idx[None] == cur[:, None]) | (slc_idx[None] == cur[:, None] - 1)
        avail = slc_start[None] <= t[:, None]
        score = jnp.where(avail, imp + jnp.where(forced, FORCE_BONUS, 0.0), -1.0)
        top_s, top_i = lax.top_k(score, n_sel)
        ks = k_slc[bi, gi, top_i]
        vs = v_slc[bi, gi, top_i]
        tok = top_i[..., None] * SLC_BLOCK + jnp.arange(SLC_BLOCK)
        smask = (tok <= t[:, None, None]) & (top_s >= 0.0)[..., None]
        s = jnp.einsum('bgrqd,bgqnkd->bgrqnk', qb, ks, preferred_element_type=f32)
        p = masked_softmax(s.reshape(B, G, R, QB, n_sel * SLC_BLOCK),
                           smask.reshape(B, G, 1, QB, n_sel * SLC_BLOCK))
        o_slc = jnp.einsum('bgrqm,bgqmd->bgrqd', p.astype(v_slc.dtype),
                           vs.reshape(B, G, QB, n_sel * SLC_BLOCK, hd))
        kw = lax.dynamic_slice_in_dim(k_win, start, WINDOW + QB, axis=2)
        vw = lax.dynamic_slice_in_dim(v_win, start, WINDOW + QB, axis=2)
        kp = start - WINDOW + jnp.arange(WINDOW + QB)
        wmask = (kp[None] <= t[:, None]) & (kp[None] > t[:, None] - WINDOW) & (kp[None] >= 0)
        s = jnp.einsum('bgrqd,bgkd->bgrqk', qb, kw, preferred_element_type=f32)
        o_win = jnp.einsum('bgrqk,bgkd->bgrqd', masked_softmax(s, wmask).astype(vw.dtype), vw)
        out = gb[..., 0:1] * o_cmp + gb[..., 1:2] * o_slc + gb[..., 2:3] * o_win
        return out.astype(qb.dtype)

    o = lax.map(block, (q_blocks, g_blocks, starts))
    o = o.transpose(1, 0, 4, 2, 3, 5).reshape(B, S, NSA_HEADS * hd)
    return o @ w_o


def peer(h, w_q, sub_keys, u_tab, v_tab):
    B, S, D = h.shape
    f32 = jnp.float32
    xt = h.reshape(-1, PEER_TOKENS, D)

    def chunk_fn(xc):
        T = xc.shape[0]
        q = (xc @ w_q).reshape(T, PEER_HEADS, 2, PEER_QDIM // 2)
        s = jnp.einsum('thcd,hcnd->thcn', q, sub_keys, preferred_element_type=f32)
        sv, si = lax.top_k(s, PEER_TOPK)
        comb = (sv[:, :, 0, :, None] + sv[:, :, 1, None, :]).reshape(T, PEER_HEADS, PEER_TOPK * PEER_TOPK)
        cv, ci = lax.top_k(comb, PEER_TOPK)
        i1 = jnp.take_along_axis(si[:, :, 0], ci // PEER_TOPK, axis=-1)
        i2 = jnp.take_along_axis(si[:, :, 1], ci % PEER_TOPK, axis=-1)
        eidx = i1 * PEER_NKEYS + i2
        w = jax.nn.softmax(cv, axis=-1)
        u = u_tab[eidx]
        a = jax.nn.gelu(jnp.einsum('td,thkd->thk', xc, u, preferred_element_type=f32)) * w
        return jnp.einsum('thk,thkd->td', a.astype(v_tab.dtype), v_tab[eidx])

    return lax.map(chunk_fn, xt).reshape(B, S, D)


def setup_inputs(seed: int = 0) -> dict:
    key = jax.random.key(seed)
    ks = jax.random.split(key, 24)
    D = D_MODEL
    nrm = lambda k, shape, std: jax.random.normal(k, shape, jnp.float32) * std
    return {
        "x": nrm(ks[0], (BATCH, SEQ, D), 1.0),
        "c": nrm(ks[1], (BATCH, D), 1.0),
        "ada_w": nrm(ks[2], (DEPTH, D, 6 * D), 0.5 * D ** -0.5),
        "ada_b": nrm(ks[3], (DEPTH, 6 * D), 0.02),
        "ln_g": 1.0 + nrm(ks[4], (DEPTH, 2, D), 0.02),
        "ln_b": nrm(ks[5], (DEPTH, 2, D), 0.02),
        "ret_w_in": nrm(ks[6], (N_A_LAYERS, D, RET_IN), D ** -0.5),
        "ret_w_o": nrm(ks[7], (N_A_LAYERS, RET_HEADS * RET_V_DIM, D), BETA * (RET_HEADS * RET_V_DIM) ** -0.5),
        "kv_ada_w": nrm(ks[8], (D, 2 * D), 0.5 * D ** -0.5),
        "kv_ada_b": nrm(ks[9], (2 * D,), 0.02),
        "nsa_w_kv": nrm(ks[10], (D, NSA_KV), D ** -0.5),
        "cmp_pe": nrm(ks[11], (2, CMP_LEN, NSA_HEAD_DIM), 0.1),
        "cmp_w1": nrm(ks[12], (2, CMP_LEN * NSA_HEAD_DIM, CMP_HIDDEN), (CMP_LEN * NSA_HEAD_DIM) ** -0.5),
        "cmp_b1": nrm(ks[13], (2, CMP_HIDDEN), 0.02),
        "cmp_w2": nrm(ks[14], (2, CMP_HIDDEN, NSA_HEAD_DIM), CMP_HIDDEN ** -0.5),
        "nsa_w_in": nrm(ks[15], (N_B_LAYERS, D, NSA_IN), D ** -0.5),
        "nsa_w_o": nrm(ks[16], (N_B_LAYERS, NSA_HEADS * NSA_HEAD_DIM, D), BETA * (NSA_HEADS * NSA_HEAD_DIM) ** -0.5),
        "peer_w_q": nrm(ks[17], (DEPTH, D, PEER_HEADS * PEER_QDIM), D ** -0.5),
        "peer_keys": nrm(ks[18], (DEPTH, PEER_HEADS, 2, PEER_NKEYS, PEER_QDIM // 2), (PEER_QDIM // 2) ** -0.5),
        "peer_u": nrm(ks[19], (DEPTH, PEER_EXPERTS, D), D ** -0.5),
        "peer_v": nrm(ks[20], (DEPTH, PEER_EXPERTS, D), BETA),
    }


def reference(x, c, ada_w, ada_b, ln_g, ln_b, ret_w_in, ret_w_o, kv_ada_w, kv_ada_b, nsa_w_kv,
              cmp_pe, cmp_w1, cmp_b1, cmp_w2, nsa_w_in, nsa_w_o, peer_w_q, peer_keys, peer_u, peer_v):
    c_act = jax.nn.silu(c)
    shared = None
    for layer in range(DEPTH):
        mod = (c_act @ ada_w[layer] + ada_b[layer])[:, None, :]
        sh1, sc1, g1, sh2, sc2, g2 = jnp.split(mod, 6, axis=-1)
        h = x * (1.0 + sc1) + sh1
        if layer < N_A_LAYERS:
            y = retention(h, ret_w_in[layer], ret_w_o[layer])
        else:
            lb = layer - N_A_LAYERS
            y = nsa(h, nsa_w_in[lb], nsa_w_o[lb], *shared)
        x = layer_norm(ALPHA * x + g1 * y, ln_g[layer, 0], ln_b[layer, 0])
        h = x * (1.0 + sc2) + sh2
        y = peer(h, peer_w_q[layer], peer_keys[layer], peer_u[layer], peer_v[layer])
        x = layer_norm(ALPHA * x + g2 * y, ln_g[layer, 1], ln_b[layer, 1])
        if layer == N_A_LAYERS - 1:
            kv_mod = (c_act @ kv_ada_w + kv_ada_b)[:, None, :]
            kv_sh, kv_sc = jnp.split(kv_mod, 2, axis=-1)
            shared = nsa_shared_kv(x * (1.0 + kv_sc) + kv_sh, nsa_w_kv, cmp_pe, cmp_w1, cmp_b1, cmp_w2)
    return x
```

```python
import functools

import numpy as np
import jax
import jax.numpy as jnp
from jax import lax
from jax.experimental import pallas as pl
from jax.experimental.pallas import tpu as pltpu

DEPTH = 4
N_A_LAYERS = DEPTH // 2
ALPHA = (2.0 * DEPTH) ** 0.25
LN_EPS = 1e-5
NEG_INF = -1e30

RET_HEADS = 4
RET_CHUNK = 128

NSA_HEADS = 16
NSA_GROUPS = 4
NSA_REP = NSA_HEADS // NSA_GROUPS
NSA_HEAD_DIM = 64
N_BRANCH = 3
CMP_STRIDE = 16
CMP_LEN = 32
SLC_BLOCK = 64
SLC_TOPK = 16
WINDOW = 512
Q_BLOCK = 128
FORCE_BONUS = 100.0
SLC_TILE = 512
WIN_KEYS = WINDOW + Q_BLOCK

PEER_HEADS = 8
PEER_NKEYS = 128
PEER_TOPK = 16

LANES = 128
VMEM_LIMIT = 56 * 1024 * 1024

F32 = jnp.float32
BF16 = jnp.bfloat16
_NT = (((1,), (1,)), ((), ()))


def _params(*sem):
    return pltpu.CompilerParams(dimension_semantics=sem, vmem_limit_bytes=VMEM_LIMIT)


def _split3(a):
    hi = a.astype(BF16)
    r1 = a - hi.astype(F32)
    mid = r1.astype(BF16)
    lo = (r1 - mid.astype(F32)).astype(BF16)
    return hi, mid, lo


def _layer_norm(z, g, b):
    mu = jnp.mean(z, axis=-1, keepdims=True)
    zc = z - mu
    var = jnp.mean(zc * zc, axis=-1, keepdims=True)
    return zc * lax.rsqrt(var + LN_EPS) * g + b


def _gelu_tanh(x):
    inner = x * (0.7978845608028654 * (1.0 + 0.044715 * (x * x)))
    return 0.5 * x * (1.0 + jnp.tanh(inner))


def _cmod_kernel(c_ref, w_ref, b_ref, o_ref):
    c = c_ref[...]
    ca = c * jax.nn.sigmoid(c)
    w = w_ref[0]
    c_hi, c_mid, c_lo = _split3(ca)
    w_hi, w_mid, w_lo = _split3(w)
    dot = lambda a, b: jnp.dot(a, b, preferred_element_type=F32)
    acc = dot(c_hi, w_hi) + (dot(c_hi, w_mid) + dot(c_mid, w_hi)) + (dot(c_hi, w_lo) + dot(c_mid, w_mid) + dot(c_lo, w_hi))
    o_ref[0] = acc + b_ref[0]


def _cmod(c, w, b):
    L, D, N = w.shape
    n_b = c.shape[0]
    B = 16
    c = jnp.zeros((B, D), F32).at[:n_b].set(c)
    tn = 512
    out = pl.pallas_call(
        _cmod_kernel,
        grid=(L, N // tn),
        in_specs=[pl.BlockSpec((B, D), lambda l, n: (0, 0)),
                  pl.BlockSpec((1, D, tn), lambda l, n: (l, 0, n)),
                  pl.BlockSpec((1, 1, tn), lambda l, n: (l, 0, n))],
        out_specs=pl.BlockSpec((1, B, tn), lambda l, n: (l, 0, n)),
        out_shape=jax.ShapeDtypeStruct((L, B, N), F32),
        compiler_params=_params("parallel", "parallel"),
        name="cmod",
    )(c, w, b.reshape(L, 1, N))
    return out[:, :n_b]


def _modmm_kernel(x_ref, sc_ref, sh_ref, w_ref, o_ref, h_ref, *, transposed):
    @pl.when(pl.program_id(2) == 0)
    def _():
        h = x_ref[0] * (1.0 + sc_ref[0]) + sh_ref[0]
        h_ref[...] = h.astype(BF16)

    if transposed:
        o = lax.dot_general(w_ref[...], h_ref[...], _NT, preferred_element_type=F32)
    else:
        o = jnp.dot(h_ref[...], w_ref[...], preferred_element_type=F32)
    o_ref[0] = o.astype(o_ref.dtype)


def _modmm(x, sc, sh, w, *, transposed, tn, out_dtype=F32, tm=512):
    B, S, D = x.shape
    N = w.shape[0] if transposed else w.shape[1]
    if transposed:
        w_spec = pl.BlockSpec((tn, D), lambda b, s, n: (n, 0))
        o_spec = pl.BlockSpec((1, tn, tm), lambda b, s, n: (b, n, s))
        o_shape = (B, N, S)
    else:
        w_spec = pl.BlockSpec((D, tn), lambda b, s, n: (0, n))
        o_spec = pl.BlockSpec((1, tm, tn), lambda b, s, n: (b, s, n))
        o_shape = (B, S, N)
    vec = pl.BlockSpec((1, 1, D), lambda b, s, n: (b, 0, 0))
    return pl.pallas_call(
        functools.partial(_modmm_kernel, transposed=transposed),
        grid=(B, S // tm, N // tn),
        in_specs=[pl.BlockSpec((1, tm, D), lambda b, s, n: (b, s, 0)), vec, vec, w_spec],
        out_specs=o_spec,
        out_shape=jax.ShapeDtypeStruct(o_shape, out_dtype),
        scratch_shapes=[pltpu.VMEM((tm, D), BF16)],
        compiler_params=_params("parallel", "parallel", "arbitrary"),
        name="modmm_t" if transposed else "modmm",
    )(x, sc, sh, w)


def _ret_kernel(q_ref, k_ref, v_ref, g_ref, cos_ref, sin_ref, decay_ref, qdec_ref, kdec_ref, cdec_ref,
                o_ref, state_ref, *, dk):
    @pl.when(pl.program_id(2) == 0)
    def _():
        state_ref[...] = jnp.zeros_like(state_ref)

    cos = cos_ref[...]
    sin = sin_ref[...]
    half = dk // 2

    def rot(t):
        x1, x2 = t[:, :half], t[:, half:]
        return jnp.concatenate([x1 * cos - x2 * sin, x1 * sin + x2 * cos], axis=-1)

    q = rot(q_ref[0])
    k = rot(k_ref[0]) * (dk ** -0.5)
    vb = v_ref[0].astype(BF16)
    qb = q.astype(BF16)
    s = lax.dot_general(qb, k.astype(BF16), _NT, preferred_element_type=F32) * decay_ref[0]
    inner = jnp.dot(s.astype(BF16), vb, preferred_element_type=F32)
    state = state_ref[...]
    cross = jnp.dot(qb, state.astype(BF16), preferred_element_type=F32) * qdec_ref[0]
    kd_t = (k * kdec_ref[0]).T.astype(BF16)
    state_ref[...] = state * cdec_ref[0] + jnp.dot(kd_t, vb, preferred_element_type=F32)
    o = inner + cross
    mu = jnp.mean(o, axis=-1, keepdims=True)
    oc = o - mu
    var = jnp.mean(oc * oc, axis=-1, keepdims=True)
    on = oc * lax.rsqrt(var + LN_EPS)
    g = g_ref[0]
    o_ref[0] = (g * jax.nn.sigmoid(g) * on).astype(o_ref.dtype)


def _retention(proj, tables):
    B, S, n_in = proj.shape
    H, C = RET_HEADS, RET_CHUNK
    dk = n_in // (6 * H)
    dv = 2 * dk
    cos, sin, decay, qdec, kdec, cdec = tables
    return pl.pallas_call(
        functools.partial(_ret_kernel, dk=dk),
        grid=(B, H, S // C),
        in_specs=[
            pl.BlockSpec((1, C, dk), lambda b, h, c: (b, c, h)),
            pl.BlockSpec((1, C, dk), lambda b, h, c: (b, c, H + h)),
            pl.BlockSpec((1, C, dv), lambda b, h, c: (b, c, H + h)),
            pl.BlockSpec((1, C, dv), lambda b, h, c: (b, c, 2 * H + h)),
            pl.BlockSpec((C, dk // 2), lambda b, h, c: (c, 0)),
            pl.BlockSpec((C, dk // 2), lambda b, h, c: (c, 0)),
            pl.BlockSpec((1, C, C), lambda b, h, c: (h, 0, 0)),
            pl.BlockSpec((1, C, dv), lambda b, h, c: (h, 0, 0)),
            pl.BlockSpec((1, C, dk), lambda b, h, c: (h, 0, 0)),
            pl.BlockSpec((1, 1, dv), lambda b, h, c: (h, 0, 0)),
        ],
        out_specs=pl.BlockSpec((1, C, dv), lambda b, h, c: (b, c, h)),
        out_shape=jax.ShapeDtypeStruct((B, S, H * dv), BF16),
        scratch_shapes=[pltpu.VMEM((dk, dv), F32)],
        compiler_params=_params("parallel", "parallel", "arbitrary"),
        name="retention",
    )(proj, proj, proj, proj, cos, sin, decay, qdec, kdec, cdec)


def _retention_tables(S, dk):
    H, C = RET_HEADS, RET_CHUNK
    dv = 2 * dk
    pos = jnp.arange(S, dtype=F32)
    theta = 1.0 / (10000.0 ** jnp.linspace(0.0, 1.0, dk // 2, dtype=F32))
    ang = pos[:, None] * theta[None, :]
    log_g = jnp.log1p(-jnp.exp2(-5.0 - jnp.arange(H, dtype=F32)))
    idx = jnp.arange(C, dtype=F32)
    diff = idx[:, None] - idx[None, :]
    decay = jnp.where(diff >= 0, jnp.exp(jnp.maximum(diff, 0.0)[None] * log_g[:, None, None]), 0.0)
    q_dec = jnp.exp((idx + 1.0)[None] * log_g[:, None])
    k_dec = jnp.exp((C - 1.0 - idx)[None] * log_g[:, None])
    c_dec = jnp.exp(C * log_g)
    return (jnp.cos(ang), jnp.sin(ang), decay,
            jnp.broadcast_to(q_dec[:, :, None], (H, C, dv)),
            jnp.broadcast_to(k_dec[:, :, None], (H, C, dk)),
            jnp.broadcast_to(c_dec[:, None, None], (H, 1, dv)))


def _oproj_ln_kernel(a_ref, w_ref, x_ref, g_ref, lg_ref, lb_ref, o_ref):
    y = jnp.dot(a_ref[0], w_ref[...], preferred_element_type=F32)
    z = ALPHA * x_ref[0] + g_ref[0] * y
    o_ref[0] = _layer_norm(z, lg_ref[...], lb_ref[...])


def _oproj_ln(a, w, x, gate, lg, lb, tm=512):
    B, S, D = x.shape
    K = a.shape[-1]
    vec = pl.BlockSpec((1, 1, D), lambda b, s: (b, 0, 0))
    par = pl.BlockSpec((1, D), lambda b, s: (0, 0))
    return pl.pallas_call(
        _oproj_ln_kernel,
        grid=(B, S // tm),
        in_specs=[pl.BlockSpec((1, tm, K), lambda b, s: (b, s, 0)),
                  pl.BlockSpec((K, D), lambda b, s: (0, 0)),
                  pl.BlockSpec((1, tm, D), lambda b, s: (b, s, 0)),
                  vec, par, par],
        out_specs=pl.BlockSpec((1, tm, D), lambda b, s: (b, s, 0)),
        out_shape=jax.ShapeDtypeStruct((B, S, D), F32),
        compiler_params=_params("parallel", "parallel"),
        name="oproj_ln",
    )(a, w, x, gate, lg.reshape(1, D), lb.reshape(1, D))


def _top_values(s, k):
    vals = []
    for _ in range(k):
        m = jnp.max(s, axis=0, keepdims=True)
        vals.append(m)
        s = jnp.where(s == m, -jnp.inf, s)
    return vals


_PEER_PAIRS = [(p, q) for p in range(PEER_TOPK) for q in range(PEER_TOPK) if (p + 1) * (q + 1) <= PEER_TOPK + 1]
_PEER_CAND_ROWS = -(-len(_PEER_PAIRS) // 8) * 8


def _peer_kernel(x_ref, sc_ref, sh_ref, g_ref, wq_ref, keys_ref, u_ref, vt_ref, lg_ref, lb_ref, o_ref,
                 h_ref, w1_ref, w2_ref, th_ref, cand_ref, acc_ref, *, te):
    e = pl.program_id(2)

    @pl.when(e == 0)
    def _route():
        h = (x_ref[0] * (1.0 + sc_ref[0]) + sh_ref[0]).astype(BF16)
        h_ref[...] = h
        q_t = lax.dot_general(wq_ref[...], h, _NT, preferred_element_type=F32).astype(BF16)
        dq = keys_ref.shape[2]
        cand_ref[...] = jnp.full(cand_ref.shape, -jnp.inf, F32)
        for hd in range(PEER_HEADS):
            s1 = jnp.dot(keys_ref[2 * hd], q_t[(2 * hd) * dq:(2 * hd + 1) * dq], preferred_element_type=F32)
            s2 = jnp.dot(keys_ref[2 * hd + 1], q_t[(2 * hd + 1) * dq:(2 * hd + 2) * dq], preferred_element_type=F32)
            a1 = _top_values(s1, PEER_TOPK)
            a2 = _top_values(s2, PEER_TOPK)
            for i, (p, q) in enumerate(_PEER_PAIRS):
                cand_ref[i:i + 1, :] = a1[p] + a2[q]
            c = _top_values(cand_ref[...], PEER_TOPK + 1)
            z = jnp.ones_like(c[0])
            for kk in range(1, PEER_TOPK):
                z = z + jnp.exp(c[kk] - c[0])
            inv_z = 1.0 / z
            w1_ref[hd] = jnp.exp(s1 - a1[0]) * inv_z
            w2_ref[hd] = jnp.exp(s2 - a2[0])
            th_ref[hd] = jnp.exp(0.5 * (c[PEER_TOPK - 1] + c[PEER_TOPK]) - c[0]) * inv_z
        acc_ref[...] = jnp.zeros_like(acc_ref)

    a_t = lax.dot_general(u_ref[...], h_ref[...], _NT, preferred_element_type=F32)
    n_i = te // PEER_NKEYS
    parts = []
    for ii in range(n_i):
        gi = e * n_i + ii
        wsum = None
        for hd in range(PEER_HEADS):
            prod = w1_ref[hd, pl.ds(gi, 1), :] * w2_ref[hd]
            term = jnp.where(prod >= th_ref[hd], prod, 0.0)
            wsum = term if wsum is None else wsum + term
        act = _gelu_tanh(a_t[ii * PEER_NKEYS:(ii + 1) * PEER_NKEYS])
        parts.append((act * wsum).astype(BF16))
    p_t = parts[0] if n_i == 1 else jnp.concatenate(parts, axis=0)
    acc_ref[...] += jnp.dot(vt_ref[...], p_t, preferred_element_type=F32)

    @pl.when(e == pl.num_programs(2) - 1)
    def _finish():
        y = acc_ref[...].T
        z = ALPHA * x_ref[0] + g_ref[0] * y
        o_ref[0] = _layer_norm(z, lg_ref[...], lb_ref[...])


def _peer(x, sc, sh, gate, wq_t, keys, u, v_t, lg, lb, tm=512, te=512):
    B, S, D = x.shape
    E = u.shape[0]
    nq = wq_t.shape[0]
    vec = pl.BlockSpec((1, 1, D), lambda b, s, e: (b, 0, 0))
    par = pl.BlockSpec((1, D), lambda b, s, e: (0, 0))
    return pl.pallas_call(
        functools.partial(_peer_kernel, te=te),
        grid=(B, S // tm, E // te),
        in_specs=[pl.BlockSpec((1, tm, D), lambda b, s, e: (b, s, 0)), vec, vec, vec,
                  pl.BlockSpec((nq, D), lambda b, s, e: (0, 0)),
                  pl.BlockSpec(keys.shape, lambda b, s, e: (0, 0, 0)),
                  pl.BlockSpec((te, D), lambda b, s, e: (e, 0)),
                  pl.BlockSpec((D, te), lambda b, s, e: (0, e)),
                  par, par],
        out_specs=pl.BlockSpec((1, tm, D), lambda b, s, e: (b, s, 0)),
        out_shape=jax.ShapeDtypeStruct((B, S, D), F32),
        scratch_shapes=[pltpu.VMEM((tm, D), BF16),
                        pltpu.VMEM((PEER_HEADS, PEER_NKEYS, tm), F32),
                        pltpu.VMEM((PEER_HEADS, PEER_NKEYS, tm), F32),
                        pltpu.VMEM((PEER_HEADS, 1, tm), F32),
                        pltpu.VMEM((_PEER_CAND_ROWS, tm), F32),
                        pltpu.VMEM((D, tm), F32)],
        compiler_params=_params("parallel", "parallel", "arbitrary"),
        name="peer",
    )(x, sc, sh, gate, wq_t, keys, u, v_t, lg.reshape(1, D), lb.reshape(1, D))


def _compress_kernel(pc_ref, pe_ref, w1_ref, b1_ref, w2_ref, w2t_ref, o_ref, ot_ref):
    pc = pc_ref[0, 0, 0]
    half = pc.shape[1]
    lo = (pc + pe_ref[0, 0:1, :]).astype(BF16)
    hi = (pc + pe_ref[0, 1:2, :]).astype(BF16)
    a = jnp.dot(lo, w1_ref[0, :half, :], preferred_element_type=F32)
    b = jnp.dot(hi, w1_ref[0, half:, :], preferred_element_type=F32)
    b_next = pltpu.roll(b, pc.shape[0] - 1, 0)
    hid = jax.nn.gelu(a + b_next + b1_ref[0]).astype(BF16)
    o_ref[0, 0, 0] = jnp.dot(hid, w2_ref[0], preferred_element_type=F32).astype(o_ref.dtype)
    ot_ref[0, 0, 0] = lax.dot_general(w2t_ref[0], hid, _NT, preferred_element_type=F32).astype(ot_ref.dtype)


def _compress(pieces, pe2, w1, b1, w2, w2t):
    _, B, G, NP, F = pieces.shape
    Hd = w1.shape[2]
    hd = w2.shape[2]
    return pl.pallas_call(
        _compress_kernel,
        grid=(2, B, G),
        in_specs=[pl.BlockSpec((1, 1, 1, NP, F), lambda c, b, g: (c, b, g, 0, 0)),
                  pl.BlockSpec((1, 2, F), lambda c, b, g: (c, 0, 0)),
                  pl.BlockSpec((1, 2 * F, Hd), lambda c, b, g: (c, 0, 0)),
                  pl.BlockSpec((1, 1, Hd), lambda c, b, g: (c, 0, 0)),
                  pl.BlockSpec((1, Hd, hd), lambda c, b, g: (c, 0, 0)),
                  pl.BlockSpec((1, hd, Hd), lambda c, b, g: (c, 0, 0))],
        out_specs=[pl.BlockSpec((1, 1, 1, NP, hd), lambda c, b, g: (c, b, g, 0, 0)),
                   pl.BlockSpec((1, 1, 1, hd, NP), lambda c, b, g: (c, b, g, 0, 0))],
        out_shape=[jax.ShapeDtypeStruct((2, B, G, NP, hd), BF16),
                   jax.ShapeDtypeStruct((2, B, G, hd, NP), BF16)],
        compiler_params=_params("parallel", "parallel", "parallel"),
        name="compress",
    )(pieces, pe2, w1, b1, w2, w2t)


def _masked_softmax_t(s, mask):
    sm = jnp.where(mask, s, NEG_INF)
    m = jnp.max(sm, axis=0, keepdims=True)
    e = jnp.where(mask, jnp.exp(sm - m), 0.0)
    return e * (1.0 / jnp.maximum(jnp.sum(e, axis=0, keepdims=True), 1e-30))


def _nsa_kernel(qt_ref, gl_ref, kc_ref, vct_ref, ks_ref, vst_ref, kw_ref, vwt_ref, c2s_ref, o_ref, sel_ref):
    R, hd, QB = NSA_REP, NSA_HEAD_DIM, Q_BLOCK
    g = pl.program_id(1)
    start = pl.program_id(2) * QB
    q4 = qt_ref[0]
    q_t = jnp.concatenate([q4[r * hd:(r + 1) * hd] for r in range(R)], axis=1).astype(BF16)
    t_q = start + lax.broadcasted_iota(jnp.int32, (1, QB), 1)
    lanes = lambda r: slice(r * QB, (r + 1) * QB)

    n_cp = kc_ref.shape[2]
    s = jnp.dot(kc_ref[0, 0], q_t, preferred_element_type=F32)
    n_idx = lax.broadcasted_iota(jnp.int32, (n_cp, 1), 0)
    cmask = (n_idx * CMP_STRIDE + (CMP_LEN - 1)) <= t_q
    p_r = [_masked_softmax_t(s[:, lanes(r)], cmask) for r in range(R)]
    p_cmp = jnp.concatenate(p_r, axis=1)
    o_cmp = jnp.dot(vct_ref[0, 0], p_cmp.astype(BF16), preferred_element_type=F32)

    psum = p_r[0]
    for r in range(1, R):
        psum = psum + p_r[r]
    c2s = c2s_ref[...]
    imp = None
    for part in _split3(psum):
        d = jnp.dot(c2s, part, preferred_element_type=F32)
        imp = d if imp is None else imp + d
    n_slc = c2s.shape[0]
    j_idx = lax.broadcasted_iota(jnp.int32, (n_slc, 1), 0)
    cur = jnp.right_shift(t_q, SLC_BLOCK.bit_length() - 1)
    forced = (j_idx == 0) | (j_idx == cur) | (j_idx == cur - 1)
    avail = (j_idx * SLC_BLOCK) <= t_q
    score = jnp.where(avail, imp + jnp.where(forced, FORCE_BONUS, 0.0), -1.0)
    work = score
    sel = jnp.zeros(score.shape, F32)
    for _ in range(min(SLC_TOPK, n_slc)):
        mx = jnp.max(work, axis=0, keepdims=True)
        first = jnp.min(jnp.where(work == mx, j_idx, n_slc), axis=0, keepdims=True)
        hit = j_idx == first
        sel = jnp.where(hit, 1.0, sel)
        work = jnp.where(hit, -jnp.inf, work)
    sel_ref[...] = jnp.where(score >= 0.0, sel, 0.0)

    per_tile = SLC_TILE // SLC_BLOCK
    row_tok = lax.broadcasted_iota(jnp.int32, (SLC_TILE, 1), 0)

    def slc_step(kt, carry, causal):
        m, l, acc = carry
        k0 = pl.multiple_of(kt * SLC_TILE, SLC_TILE)
        s = jnp.dot(ks_ref[0, 0, pl.ds(k0, SLC_TILE), :], q_t, preferred_element_type=F32)
        sel_rows = sel_ref[pl.ds(pl.multiple_of(kt * per_tile, per_tile), per_tile), :]
        mask = jnp.concatenate(
            [jnp.broadcast_to(sel_rows[a:a + 1, :], (SLC_BLOCK, QB)) for a in range(per_tile)], axis=0) > 0.5
        if causal:
            mask = mask & ((k0 + row_tok) <= t_q)
        ms, ls, ps, alphas = [], [], [], []
        for r in range(R):
            sm = jnp.where(mask, s[:, lanes(r)], NEG_INF)
            m_new = jnp.maximum(m[:, lanes(r)], jnp.max(sm, axis=0, keepdims=True))
            p = jnp.where(mask, jnp.exp(sm - m_new), 0.0)
            alpha = jnp.exp(m[:, lanes(r)] - m_new)
            ms.append(m_new)
            ls.append(l[:, lanes(r)] * alpha + jnp.sum(p, axis=0, keepdims=True))
            ps.append(p.astype(BF16))
            alphas.append(alpha)
        pv = jnp.dot(vst_ref[0, 0, kt], jnp.concatenate(ps, axis=1), preferred_element_type=F32)
        return (jnp.concatenate(ms, axis=1), jnp.concatenate(ls, axis=1),
                acc * jnp.concatenate(alphas, axis=1) + pv)

    last = start // SLC_TILE
    init = (jnp.full((1, R * QB), NEG_INF, F32), jnp.zeros((1, R * QB), F32), jnp.zeros((hd, R * QB), F32))
    carry = lax.fori_loop(0, last, lambda kt, c: slc_step(kt, c, False), init)
    _, l, acc = slc_step(last, carry, True)
    o_slc = acc * (1.0 / jnp.maximum(l, 1e-30))

    base = pl.multiple_of(jnp.maximum(start - WINDOW, 0), QB)
    s = jnp.dot(kw_ref[0, 0, pl.ds(base, WIN_KEYS), :], q_t, preferred_element_type=F32)
    kp = base + lax.broadcasted_iota(jnp.int32, (WIN_KEYS, 1), 0)
    wmask = (kp <= t_q) & (kp > t_q - WINDOW)
    p_win = jnp.concatenate([_masked_softmax_t(s[:, lanes(r)], wmask) for r in range(R)], axis=1).astype(BF16)
    o_win = None
    for a in range(WIN_KEYS // QB):
        d = jnp.dot(vwt_ref[0, 0, base // QB + a], p_win[a * QB:(a + 1) * QB], preferred_element_type=F32)
        o_win = d if o_win is None else o_win + d

    outs = []
    for r in range(R):
        row = (g * R + r) * N_BRANCH
        gate = [jax.nn.sigmoid(gl_ref[0, pl.ds(row + br, 1), :]) for br in range(N_BRANCH)]
        outs.append(gate[0] * o_cmp[:, lanes(r)] + gate[1] * o_slc[:, lanes(r)] + gate[2] * o_win[:, lanes(r)])
    o_ref[0] = jnp.concatenate(outs, axis=0).T.astype(o_ref.dtype)


def _nsa_attention(proj_t, shared, c2s_t):
    k_cmp, v_cmp_t, k_slc, v_slc_t, k_win, v_win_t = shared
    B, _, S = proj_t.shape
    G, R, hd, QB = NSA_GROUPS, NSA_REP, NSA_HEAD_DIM, Q_BLOCK
    n_cp = k_cmp.shape[2]
    n_slc = S // SLC_BLOCK
    gate_blk = (NSA_HEADS * hd) // LANES
    whole = lambda shape: pl.BlockSpec((1, 1) + shape, lambda b, g, q: (b, g) + (0,) * len(shape))
    return pl.pallas_call(
        _nsa_kernel,
        grid=(B, G, S // QB),
        in_specs=[pl.BlockSpec((1, R * hd, QB), lambda b, g, q: (b, g, q)),
                  pl.BlockSpec((1, LANES, QB), lambda b, g, q: (b, gate_blk, q)),
                  whole((n_cp, hd)), whole((hd, n_cp)),
                  whole((S, hd)), whole((S // SLC_TILE, hd, SLC_TILE)),
                  whole((S, hd)), whole((S // QB, hd, QB)),
                  pl.BlockSpec((n_slc, n_cp), lambda b, g, q: (0, 0))],
        out_specs=pl.BlockSpec((1, QB, R * hd), lambda b, g, q: (b, q, g)),
        out_shape=jax.ShapeDtypeStruct((B, S, NSA_HEADS * hd), BF16),
        scratch_shapes=[pltpu.VMEM((n_slc, QB), F32)],
        compiler_params=_params("parallel", "parallel", "arbitrary"),
        name="nsa_attention",
    )(proj_t, proj_t, k_cmp, v_cmp_t, k_slc, v_slc_t, k_win, v_win_t, c2s_t)


def _cmp_to_slc_t(n_cp, n_slc):
    i = np.arange(n_cp)[None, :] * CMP_STRIDE
    j = np.arange(n_slc)[:, None] * SLC_BLOCK
    ov = np.clip(np.minimum(i + CMP_LEN, j + SLC_BLOCK) - np.maximum(i, j), 0, None) / CMP_LEN
    ov[:, n_cp - 1] = 0.0
    return jnp.asarray(ov, dtype=BF16)


def _nsa_shared_kv(x, kv_sc, kv_sh, w_kv, cmp_pe, cmp_w1, cmp_b1, cmp_w2):
    B, S, _ = x.shape
    G, hd = NSA_GROUPS, NSA_HEAD_DIM
    kv = _modmm(x, kv_sc, kv_sh, w_kv.astype(BF16), transposed=False, tn=512)
    kv = kv.reshape(B, S, N_BRANCH, 2, G, hd)
    NP = S // CMP_STRIDE
    pieces = kv[:, :, 0].reshape(B, NP, CMP_STRIDE, 2, G, hd).transpose(3, 0, 4, 1, 2, 5)
    pieces = pieces.reshape(2, B, G, NP, CMP_STRIDE * hd)
    comp, comp_t = _compress(pieces, cmp_pe.reshape(2, 2, CMP_STRIDE * hd), cmp_w1.astype(BF16),
                             cmp_b1[:, None, :], cmp_w2.astype(BF16), cmp_w2.transpose(0, 2, 1).astype(BF16))
    to_k = lambda t: t.transpose(0, 2, 1, 3).astype(BF16)

    def to_vt(t, tile):
        return t.reshape(B, S // tile, tile, G, hd).transpose(0, 3, 1, 4, 2).astype(BF16)

    return (comp[0], comp_t[1], to_k(kv[:, :, 1, 0]), to_vt(kv[:, :, 1, 1], SLC_TILE),
            to_k(kv[:, :, 2, 0]), to_vt(kv[:, :, 2, 1], Q_BLOCK))


def kernel(x, c, ada_w, ada_b, ln_g, ln_b, ret_w_in, ret_w_o, kv_ada_w, kv_ada_b, nsa_w_kv, cmp_pe, cmp_w1, cmp_b1, cmp_w2, nsa_w_in, nsa_w_o, peer_w_q, peer_keys, peer_u, peer_v):
    B, S, D = x.shape
    mods = _cmod(c, ada_w, ada_b)
    kv_mod = _cmod(c, kv_ada_w[None], kv_ada_b[None])[0]
    ret_tables = _retention_tables(S, D // RET_HEADS)
    hd = NSA_HEAD_DIM
    n_gate = NSA_HEADS * N_BRANCH
    c2s_t = _cmp_to_slc_t(S // CMP_STRIDE, S // SLC_BLOCK)
    shared = None
    for layer in range(DEPTH):
        sh1, sc1, g1, sh2, sc2, g2 = [m[:, None, :] for m in jnp.split(mods[layer], 6, axis=-1)]
        if layer < N_A_LAYERS:
            proj = _modmm(x, sc1, sh1, ret_w_in[layer].astype(BF16), transposed=False, tn=512)
            a = _retention(proj, ret_tables)
            w_o = ret_w_o[layer]
        else:
            lb = layer - N_A_LAYERS
            w_in = nsa_w_in[lb]
            w_t = jnp.concatenate([w_in[:, :NSA_HEADS * hd].T * (hd ** -0.5), w_in[:, NSA_HEADS * hd:].T,
                                   jnp.zeros((LANES - n_gate, D), F32)], axis=0).astype(BF16)
            proj_t = _modmm(x, sc1, sh1, w_t, transposed=True, tn=384)
            a = _nsa_attention(proj_t, shared, c2s_t)
            w_o = nsa_w_o[lb]
        x = _oproj_ln(a, w_o.astype(BF16), x, g1, ln_g[layer, 0], ln_b[layer, 0])
        keys = peer_keys[layer].reshape(PEER_HEADS * 2, PEER_NKEYS, -1).astype(BF16)
        x = _peer(x, sc2, sh2, g2, peer_w_q[layer].T.astype(BF16), keys, peer_u[layer].astype(BF16),
                  peer_v[layer].T.astype(BF16), ln_g[layer, 1], ln_b[layer, 1])
        if layer == N_A_LAYERS - 1:
            kv_sh, kv_sc = [m[:, None, :] for m in jnp.split(kv_mod, 2, axis=-1)]
            shared = _nsa_shared_kv(x, kv_sc, kv_sh, nsa_w_kv, cmp_pe, cmp_w1, cmp_b1, cmp_w2)
    return x
```

```python
import functools

import numpy as np
import jax
import jax.numpy as jnp
from jax import lax
from jax.experimental import pallas as pl
from jax.experimental.pallas import tpu as pltpu

DEPTH = 4
N_A_LAYERS = DEPTH // 2
ALPHA = (2.0 * DEPTH) ** 0.25
LN_EPS = 1e-5
NEG_INF = -1e30
MASKED = 2.0 * NEG_INF
LOG2E = 1.4426950408889634

RET_HEADS = 4
RET_CHUNK = 128

NSA_HEADS = 16
NSA_GROUPS = 4
NSA_REP = NSA_HEADS // NSA_GROUPS
NSA_HEAD_DIM = 64
N_BRANCH = 3
CMP_STRIDE = 16
CMP_LEN = 32
SLC_BLOCK = 64
SLC_TOPK = 16
WINDOW = 512
Q_BLOCK = 128
FORCE_BONUS = 100.0
SLC_TILE = 512
WIN_KEYS = WINDOW + Q_BLOCK

PEER_HEADS = 8
PEER_NKEYS = 128
PEER_TOPK = 16

LANES = 128
VMEM_LIMIT = 56 * 1024 * 1024

F32 = jnp.float32
BF16 = jnp.bfloat16
_NT = (((1,), (1,)), ((), ()))


def _params(*sem):
    return pltpu.CompilerParams(dimension_semantics=sem, vmem_limit_bytes=VMEM_LIMIT)


def _split3(a):
    hi = a.astype(BF16)
    r1 = a - hi.astype(F32)
    mid = r1.astype(BF16)
    lo = (r1 - mid.astype(F32)).astype(BF16)
    return hi, mid, lo


def _layer_norm(z, g, b):
    mu = jnp.mean(z, axis=-1, keepdims=True)
    zc = z - mu
    var = jnp.mean(zc * zc, axis=-1, keepdims=True)
    return zc * lax.rsqrt(var + LN_EPS) * g + b


def _gelu_tanh(x):
    inner = x * (0.7978845608028654 * (1.0 + 0.044715 * (x * x)))
    return 0.5 * x * (1.0 + jnp.tanh(inner))


def _cmod_kernel(c_ref, w_ref, b_ref, o_ref):
    c = c_ref[...]
    ca = c * jax.nn.sigmoid(c)
    w = w_ref[0]
    c_hi, c_mid, c_lo = _split3(ca)
    w_hi, w_mid, w_lo = _split3(w)
    dot = lambda a, b: jnp.dot(a, b, preferred_element_type=F32)
    acc = dot(c_hi, w_hi) + (dot(c_hi, w_mid) + dot(c_mid, w_hi)) + (dot(c_hi, w_lo) + dot(c_mid, w_mid) + dot(c_lo, w_hi))
    o_ref[0] = acc + b_ref[0]


def _cmod(c, w, b):
    L, D, N = w.shape
    n_b = c.shape[0]
    B = 16
    c = jnp.zeros((B, D), F32).at[:n_b].set(c)
    tn = 512
    out = pl.pallas_call(
        _cmod_kernel,
        grid=(L, N // tn),
        in_specs=[pl.BlockSpec((B, D), lambda l, n: (0, 0)),
                  pl.BlockSpec((1, D, tn), lambda l, n: (l, 0, n)),
                  pl.BlockSpec((1, 1, tn), lambda l, n: (l, 0, n))],
        out_specs=pl.BlockSpec((1, B, tn), lambda l, n: (l, 0, n)),
        out_shape=jax.ShapeDtypeStruct((L, B, N), F32),
        compiler_params=_params("parallel", "parallel"),
        name="cmod",
    )(c, w, b.reshape(L, 1, N))
    return out[:, :n_b]


def _modmm_kernel(x_ref, sc_ref, sh_ref, w_ref, o_ref, h_ref, *, transposed):
    @pl.when(pl.program_id(2) == 0)
    def _():
        h = x_ref[0] * (1.0 + sc_ref[0]) + sh_ref[0]
        h_ref[...] = h.astype(BF16)

    if transposed:
        o = lax.dot_general(w_ref[...], h_ref[...], _NT, preferred_element_type=F32)
    else:
        o = jnp.dot(h_ref[...], w_ref[...], preferred_element_type=F32)
    o_ref[0] = o.astype(o_ref.dtype)


def _modmm(x, sc, sh, w, *, transposed, tn, out_dtype=F32, tm=512):
    B, S, D = x.shape
    N = w.shape[0] if transposed else w.shape[1]
    if transposed:
        w_spec = pl.BlockSpec((tn, D), lambda b, s, n: (n, 0))
        o_spec = pl.BlockSpec((1, tn, tm), lambda b, s, n: (b, n, s))
        o_shape = (B, N, S)
    else:
        w_spec = pl.BlockSpec((D, tn), lambda b, s, n: (0, n))
        o_spec = pl.BlockSpec((1, tm, tn), lambda b, s, n: (b, s, n))
        o_shape = (B, S, N)
    vec = pl.BlockSpec((1, 1, D), lambda b, s, n: (b, 0, 0))
    return pl.pallas_call(
        functools.partial(_modmm_kernel, transposed=transposed),
        grid=(B, S // tm, N // tn),
        in_specs=[pl.BlockSpec((1, tm, D), lambda b, s, n: (b, s, 0)), vec, vec, w_spec],
        out_specs=o_spec,
        out_shape=jax.ShapeDtypeStruct(o_shape, out_dtype),
        scratch_shapes=[pltpu.VMEM((tm, D), BF16)],
        compiler_params=_params("parallel", "parallel", "arbitrary"),
        name="modmm_t" if transposed else "modmm",
    )(x, sc, sh, w)


def _ret_kernel(q_ref, k_ref, v_ref, g_ref, cos_ref, sin_ref, decay_ref, qdec_ref, kdec_ref, cdec_ref,
                o_ref, state_ref, *, dk):
    @pl.when(pl.program_id(2) == 0)
    def _():
        state_ref[...] = jnp.zeros_like(state_ref)

    cos = cos_ref[...]
    sin = sin_ref[...]
    half = dk // 2

    def rot(t):
        x1, x2 = t[:, :half], t[:, half:]
        return jnp.concatenate([x1 * cos - x2 * sin, x1 * sin + x2 * cos], axis=-1)

    q = rot(q_ref[0])
    k = rot(k_ref[0]) * (dk ** -0.5)
    vb = v_ref[0].astype(BF16)
    qb = q.astype(BF16)
    s = lax.dot_general(qb, k.astype(BF16), _NT, preferred_element_type=F32) * decay_ref[0]
    inner = jnp.dot(s.astype(BF16), vb, preferred_element_type=F32)
    state = state_ref[...]
    cross = jnp.dot(qb, state.astype(BF16), preferred_element_type=F32) * qdec_ref[0]
    kd_t = (k * kdec_ref[0]).T.astype(BF16)
    state_ref[...] = state * cdec_ref[0] + jnp.dot(kd_t, vb, preferred_element_type=F32)
    o = inner + cross
    mu = jnp.mean(o, axis=-1, keepdims=True)
    oc = o - mu
    var = jnp.mean(oc * oc, axis=-1, keepdims=True)
    on = oc * lax.rsqrt(var + LN_EPS)
    g = g_ref[0]
    o_ref[0] = (g * jax.nn.sigmoid(g) * on).astype(o_ref.dtype)


def _retention(proj, tables):
    B, S, n_in = proj.shape
    H, C = RET_HEADS, RET_CHUNK
    dk = n_in // (6 * H)
    dv = 2 * dk
    cos, sin, decay, qdec, kdec, cdec = tables
    return pl.pallas_call(
        functools.partial(_ret_kernel, dk=dk),
        grid=(B, H, S // C),
        in_specs=[
            pl.BlockSpec((1, C, dk), lambda b, h, c: (b, c, h)),
            pl.BlockSpec((1, C, dk), lambda b, h, c: (b, c, H + h)),
            pl.BlockSpec((1, C, dv), lambda b, h, c: (b, c, H + h)),
            pl.BlockSpec((1, C, dv), lambda b, h, c: (b, c, 2 * H + h)),
            pl.BlockSpec((C, dk // 2), lambda b, h, c: (c, 0)),
            pl.BlockSpec((C, dk // 2), lambda b, h, c: (c, 0)),
            pl.BlockSpec((1, C, C), lambda b, h, c: (h, 0, 0)),
            pl.BlockSpec((1, C, dv), lambda b, h, c: (h, 0, 0)),
            pl.BlockSpec((1, C, dk), lambda b, h, c: (h, 0, 0)),
            pl.BlockSpec((1, 1, dv), lambda b, h, c: (h, 0, 0)),
        ],
        out_specs=pl.BlockSpec((1, C, dv), lambda b, h, c: (b, c, h)),
        out_shape=jax.ShapeDtypeStruct((B, S, H * dv), BF16),
        scratch_shapes=[pltpu.VMEM((dk, dv), F32)],
        compiler_params=_params("parallel", "parallel", "arbitrary"),
        name="retention",
    )(proj, proj, proj, proj, cos, sin, decay, qdec, kdec, cdec)


def _retention_tables(S, dk):
    H, C = RET_HEADS, RET_CHUNK
    dv = 2 * dk
    pos = jnp.arange(S, dtype=F32)
    theta = 1.0 / (10000.0 ** jnp.linspace(0.0, 1.0, dk // 2, dtype=F32))
    ang = pos[:, None] * theta[None, :]
    log_g = jnp.log1p(-jnp.exp2(-5.0 - jnp.arange(H, dtype=F32)))
    idx = jnp.arange(C, dtype=F32)
    diff = idx[:, None] - idx[None, :]
    decay = jnp.where(diff >= 0, jnp.exp(jnp.maximum(diff, 0.0)[None] * log_g[:, None, None]), 0.0)
    q_dec = jnp.exp((idx + 1.0)[None] * log_g[:, None])
    k_dec = jnp.exp((C - 1.0 - idx)[None] * log_g[:, None])
    c_dec = jnp.exp(C * log_g)
    return (jnp.cos(ang), jnp.sin(ang), decay,
            jnp.broadcast_to(q_dec[:, :, None], (H, C, dv)),
            jnp.broadcast_to(k_dec[:, :, None], (H, C, dk)),
            jnp.broadcast_to(c_dec[:, None, None], (H, 1, dv)))


def _oproj_ln_kernel(a_ref, w_ref, x_ref, g_ref, lg_ref, lb_ref, o_ref):
    y = jnp.dot(a_ref[0], w_ref[...], preferred_element_type=F32)
    z = ALPHA * x_ref[0] + g_ref[0] * y
    o_ref[0] = _layer_norm(z, lg_ref[...], lb_ref[...])


def _oproj_ln(a, w, x, gate, lg, lb, tm=512):
    B, S, D = x.shape
    K = a.shape[-1]
    vec = pl.BlockSpec((1, 1, D), lambda b, s: (b, 0, 0))
    par = pl.BlockSpec((1, D), lambda b, s: (0, 0))
    return pl.pallas_call(
        _oproj_ln_kernel,
        grid=(B, S // tm),
        in_specs=[pl.BlockSpec((1, tm, K), lambda b, s: (b, s, 0)),
                  pl.BlockSpec((K, D), lambda b, s: (0, 0)),
                  pl.BlockSpec((1, tm, D), lambda b, s: (b, s, 0)),
                  vec, par, par],
        out_specs=pl.BlockSpec((1, tm, D), lambda b, s: (b, s, 0)),
        out_shape=jax.ShapeDtypeStruct((B, S, D), F32),
        compiler_params=_params("parallel", "parallel"),
        name="oproj_ln",
    )(a, w, x, gate, lg.reshape(1, D), lb.reshape(1, D))


def _top_values(s, k):
    vals = []
    for _ in range(k):
        m = jnp.max(s, axis=0, keepdims=True)
        vals.append(m)
        s = jnp.where(s == m, -jnp.inf, s)
    return vals


_PEER_PAIRS = [(p, q) for p in range(PEER_TOPK) for q in range(PEER_TOPK) if (p + 1) * (q + 1) <= PEER_TOPK + 1]
_PEER_CAND_ROWS = -(-len(_PEER_PAIRS) // 8) * 8


def _peer_kernel(x_ref, sc_ref, sh_ref, g_ref, wq_ref, keys_ref, u_ref, vt_ref, lg_ref, lb_ref, o_ref,
                 h_ref, w1_ref, w2_ref, th_ref, cand_ref, a_ref, p_ref, acc_ref, *, te, sub):
    e = pl.program_id(2)
    tm = h_ref.shape[0]

    @pl.when(e == 0)
    def _route():
        h = (x_ref[0] * (1.0 + sc_ref[0]) + sh_ref[0]).astype(BF16)
        h_ref[...] = h
        q_t = lax.dot_general(wq_ref[...], h, _NT, preferred_element_type=F32).astype(BF16)
        dq = keys_ref.shape[2]
        cand_ref[...] = jnp.full(cand_ref.shape, -jnp.inf, F32)
        for hd in range(PEER_HEADS):
            s1_all = jnp.dot(keys_ref[2 * hd], q_t[(2 * hd) * dq:(2 * hd + 1) * dq], preferred_element_type=F32)
            s2_all = jnp.dot(keys_ref[2 * hd + 1], q_t[(2 * hd + 1) * dq:(2 * hd + 2) * dq], preferred_element_type=F32)
            for lt in range(tm // LANES):
                ln = slice(lt * LANES, (lt + 1) * LANES)
                s1, s2 = s1_all[:, ln], s2_all[:, ln]
                a1 = _top_values(s1, PEER_TOPK)
                a2 = _top_values(s2, PEER_TOPK)
                for i, (p, q) in enumerate(_PEER_PAIRS):
                    cand_ref[i:i + 1, ln] = a1[p] + a2[q]
                c = _top_values(cand_ref[:, ln], PEER_TOPK + 1)
                z = jnp.ones_like(c[0])
                for kk in range(1, PEER_TOPK):
                    z = z + jnp.exp(c[kk] - c[0])
                inv_z = 1.0 / z
                w1_ref[hd, :, ln] = jnp.exp(s1 - a1[0]) * inv_z
                w2_ref[hd, :, ln] = jnp.exp(s2 - a2[0])
                th_ref[hd, :, ln] = jnp.exp(0.5 * (c[PEER_TOPK - 1] + c[PEER_TOPK]) - c[0]) * inv_z
        acc_ref[...] = jnp.zeros_like(acc_ref)

    per_sub = sub // PEER_NKEYS
    for sb in range(te // sub):
        a_ref[sb] = lax.dot_general(u_ref[sb * sub:(sb + 1) * sub, :], h_ref[...], _NT,
                                    preferred_element_type=F32)
    total = None
    for sb in range(te // sub):
        for ii in range(per_sub):
            rows = slice(ii * PEER_NKEYS, (ii + 1) * PEER_NKEYS)
            gi = e * (te // PEER_NKEYS) + sb * per_sub + ii
            wsum = None
            for hd in range(PEER_HEADS):
                prod = w1_ref[hd, pl.ds(gi, 1), :] * w2_ref[hd]
                term = jnp.where(prod >= th_ref[hd], prod, 0.0)
                wsum = term if wsum is None else wsum + term
            p_ref[sb, rows, :] = (_gelu_tanh(a_ref[sb, rows, :]) * wsum).astype(BF16)
        d = jnp.dot(vt_ref[:, sb * sub:(sb + 1) * sub], p_ref[sb], preferred_element_type=F32)
        total = d if total is None else total + d
    acc_ref[...] += total

    @pl.when(e == pl.num_programs(2) - 1)
    def _finish():
        y = acc_ref[...].T
        z = ALPHA * x_ref[0] + g_ref[0] * y
        o_ref[0] = _layer_norm(z, lg_ref[...], lb_ref[...])


def _peer(x, sc, sh, gate, wq_t, keys, u, v_t, lg, lb, tm=512, te=1024, sub=256):
    B, S, D = x.shape
    E = u.shape[0]
    nq = wq_t.shape[0]
    vec = pl.BlockSpec((1, 1, D), lambda b, s, e: (b, 0, 0))
    par = pl.BlockSpec((1, D), lambda b, s, e: (0, 0))
    return pl.pallas_call(
        functools.partial(_peer_kernel, te=te, sub=sub),
        grid=(B, S // tm, E // te),
        in_specs=[pl.BlockSpec((1, tm, D), lambda b, s, e: (b, s, 0)), vec, vec, vec,
                  pl.BlockSpec((nq, D), lambda b, s, e: (0, 0)),
                  pl.BlockSpec(keys.shape, lambda b, s, e: (0, 0, 0)),
                  pl.BlockSpec((te, D), lambda b, s, e: (e, 0)),
                  pl.BlockSpec((D, te), lambda b, s, e: (0, e)),
                  par, par],
        out_specs=pl.BlockSpec((1, tm, D), lambda b, s, e: (b, s, 0)),
        out_shape=jax.ShapeDtypeStruct((B, S, D), F32),
        scratch_shapes=[pltpu.VMEM((tm, D), BF16),
                        pltpu.VMEM((PEER_HEADS, PEER_NKEYS, tm), F32),
                        pltpu.VMEM((PEER_HEADS, PEER_NKEYS, tm), F32),
                        pltpu.VMEM((PEER_HEADS, 1, tm), F32),
                        pltpu.VMEM((_PEER_CAND_ROWS, tm), F32),
                        pltpu.VMEM((te // sub, sub, tm), F32),
                        pltpu.VMEM((te // sub, sub, tm), BF16),
                        pltpu.VMEM((D, tm), F32)],
        compiler_params=_params("parallel", "parallel", "arbitrary"),
        name="peer",
    )(x, sc, sh, gate, wq_t, keys, u, v_t, lg.reshape(1, D), lb.reshape(1, D))


def _compress_kernel(pc_ref, pe_ref, w1_ref, b1_ref, w2_ref, w2t_ref, o_ref, ot_ref):
    pc = pc_ref[0, 0, 0]
    half = pc.shape[1]
    lo = (pc + pe_ref[0, 0:1, :]).astype(BF16)
    hi = (pc + pe_ref[0, 1:2, :]).astype(BF16)
    a = jnp.dot(lo, w1_ref[0, :half, :], preferred_element_type=F32)
    b = jnp.dot(hi, w1_ref[0, half:, :], preferred_element_type=F32)
    b_next = pltpu.roll(b, pc.shape[0] - 1, 0)
    hid = jax.nn.gelu(a + b_next + b1_ref[0]).astype(BF16)
    o_ref[0, 0, 0] = jnp.dot(hid, w2_ref[0], preferred_element_type=F32).astype(o_ref.dtype)
    ot_ref[0, 0, 0] = lax.dot_general(w2t_ref[0], hid, _NT, preferred_element_type=F32).astype(ot_ref.dtype)


def _compress(pieces, pe2, w1, b1, w2, w2t):
    _, B, G, NP, F = pieces.shape
    Hd = w1.shape[2]
    hd = w2.shape[2]
    return pl.pallas_call(
        _compress_kernel,
        grid=(2, B, G),
        in_specs=[pl.BlockSpec((1, 1, 1, NP, F), lambda c, b, g: (c, b, g, 0, 0)),
                  pl.BlockSpec((1, 2, F), lambda c, b, g: (c, 0, 0)),
                  pl.BlockSpec((1, 2 * F, Hd), lambda c, b, g: (c, 0, 0)),
                  pl.BlockSpec((1, 1, Hd), lambda c, b, g: (c, 0, 0)),
                  pl.BlockSpec((1, Hd, hd), lambda c, b, g: (c, 0, 0)),
                  pl.BlockSpec((1, hd, Hd), lambda c, b, g: (c, 0, 0))],
        out_specs=[pl.BlockSpec((1, 1, 1, NP, hd), lambda c, b, g: (c, b, g, 0, 0)),
                   pl.BlockSpec((1, 1, 1, hd, NP), lambda c, b, g: (c, b, g, 0, 0))],
        out_shape=[jax.ShapeDtypeStruct((2, B, G, NP, hd), BF16),
                   jax.ShapeDtypeStruct((2, B, G, hd, NP), BF16)],
        compiler_params=_params("parallel", "parallel", "parallel"),
        name="compress",
    )(pieces, pe2, w1, b1, w2, w2t)


def _softmax_t(s, mask):
    sm = jnp.where(mask, s, MASKED)
    m = jnp.maximum(jnp.max(sm, axis=0, keepdims=True), NEG_INF)
    e = jnp.exp2(sm - m)
    return e, 1.0 / jnp.maximum(jnp.sum(e, axis=0, keepdims=True), 1e-30)


def _nsa_kernel(qt_ref, gl_ref, kc_ref, vct_ref, ks_ref, vst_ref, kw_ref, vwt_ref, c2s_ref, o_ref,
                sel_ref, sc_ref, sw_ref, sa_ref, sb_ref, pa_ref, pb_ref):
    R, hd, QB = NSA_REP, NSA_HEAD_DIM, Q_BLOCK
    g = pl.program_id(1)
    start = pl.program_id(2) * QB
    q4 = qt_ref[0] * LOG2E
    q_t = jnp.concatenate([q4[r * hd:(r + 1) * hd] for r in range(R)], axis=1).astype(BF16)
    t_q = start + lax.broadcasted_iota(jnp.int32, (1, QB), 1)
    lanes = lambda r: slice(r * QB, (r + 1) * QB)
    dot = lambda a, b: jnp.dot(a, b, preferred_element_type=F32)
    n_tiles = ks_ref.shape[2] // SLC_TILE

    def k_slc_tile(kt):
        return ks_ref[0, 0, pl.ds(pl.multiple_of(kt * SLC_TILE, SLC_TILE), SLC_TILE), :]

    base = pl.multiple_of(jnp.maximum(start - WINDOW, 0), QB)
    sc_ref[...] = dot(kc_ref[0, 0], q_t)
    sw_ref[...] = dot(kw_ref[0, 0, pl.ds(base, WIN_KEYS), :], q_t)
    sa_ref[...] = dot(k_slc_tile(0), q_t)

    n_cp = kc_ref.shape[2]
    n_idx = lax.broadcasted_iota(jnp.int32, (n_cp, 1), 0)
    cmask = (n_idx * CMP_STRIDE + (CMP_LEN - 1)) <= t_q
    v_cmp_t = vct_ref[0, 0]
    o_cmp, psum = [], None
    for r in range(R):
        e, inv = _softmax_t(sc_ref[:, lanes(r)], cmask)
        p = e * inv
        o_cmp.append(dot(v_cmp_t, p.astype(BF16)))
        psum = p if psum is None else psum + p

    kp = base + lax.broadcasted_iota(jnp.int32, (WIN_KEYS, 1), 0)
    wmask = (kp <= t_q) & (kp > t_q - WINDOW)
    o_win = []
    for r in range(R):
        e, inv = _softmax_t(sw_ref[:, lanes(r)], wmask)
        e = e.astype(BF16)
        acc = None
        for a in range(WIN_KEYS // QB):
            d = dot(vwt_ref[0, 0, base // QB + a], e[a * QB:(a + 1) * QB])
            acc = d if acc is None else acc + d
        o_win.append(acc * inv)

    c2s = c2s_ref[...]
    imp = None
    for part in _split3(psum):
        d = dot(c2s, part)
        imp = d if imp is None else imp + d
    n_slc = c2s.shape[0]
    j_idx = lax.broadcasted_iota(jnp.int32, (n_slc, 1), 0)
    cur = jnp.right_shift(t_q, SLC_BLOCK.bit_length() - 1)
    forced = (j_idx == 0) | (j_idx == cur) | (j_idx == cur - 1)
    avail = (j_idx * SLC_BLOCK) <= t_q
    score = jnp.where(avail, imp + jnp.where(forced, FORCE_BONUS, 0.0), -1.0)
    work = score
    sel = jnp.zeros(score.shape, F32)
    for _ in range(min(SLC_TOPK, n_slc)):
        mx = jnp.max(work, axis=0, keepdims=True)
        first = jnp.min(jnp.where(work == mx, j_idx, n_slc), axis=0, keepdims=True)
        hit = j_idx == first
        sel = jnp.where(hit, 1.0, sel)
        work = jnp.where(hit, -jnp.inf, work)
    sel_ref[...] = jnp.where(score >= 0.0, sel, 0.0)

    per_tile = SLC_TILE // SLC_BLOCK
    row_tok = lax.broadcasted_iota(jnp.int32, (SLC_TILE, 1), 0)
    last = start // SLC_TILE

    def tile_mask(kt, causal):
        blk = pl.multiple_of(jnp.minimum(kt, n_tiles - 1) * per_tile, per_tile)
        sel_rows = sel_ref[pl.ds(blk, per_tile), :]
        mask = jnp.concatenate(
            [jnp.broadcast_to(sel_rows[a:a + 1, :], (SLC_BLOCK, QB)) for a in range(per_tile)], axis=0) > 0.5
        if causal:
            mask = mask & ((kt * SLC_TILE + row_tok) <= t_q)
        return mask

    def softmax_update(s_buf, p_buf, mask, m, l):
        m_out, l_out, alphas = [], [], []
        for r in range(R):
            sm = jnp.where(mask, s_buf[:, lanes(r)], MASKED)
            m_new = jnp.maximum(m[r], jnp.max(sm, axis=0, keepdims=True))
            p = jnp.exp2(sm - m_new)
            alphas.append(jnp.exp2(m[r] - m_new))
            p_buf[:, lanes(r)] = p.astype(BF16)
            m_out.append(m_new)
            l_out.append(l[r] * alphas[r] + jnp.sum(p, axis=0, keepdims=True))
        return tuple(m_out), tuple(l_out), jnp.concatenate(alphas, axis=1)

    v_tile_t = lambda kt: vst_ref[0, 0, jnp.clip(kt, 0, n_tiles - 1)]

    def pair_step(t0, carry, causal):
        m, l, acc, alpha_prev = carry
        sb_ref[...] = dot(k_slc_tile(jnp.minimum(t0 + 1, n_tiles - 1)), q_t)
        m, l, alpha0 = softmax_update(sa_ref, pa_ref, tile_mask(t0, causal), m, l)
        acc = acc * alpha_prev + dot(v_tile_t(t0 - 1), pb_ref[...])
        sa_ref[...] = dot(k_slc_tile(jnp.minimum(t0 + 2, n_tiles - 1)), q_t)
        m, l, alpha1 = softmax_update(sb_ref, pb_ref, tile_mask(t0 + 1, causal), m, l)
        acc = acc * alpha0 + dot(v_tile_t(t0), pa_ref[...])
        return m, l, acc, alpha1

    pb_ref[...] = jnp.zeros_like(pb_ref)
    init = (tuple(jnp.full((1, QB), NEG_INF, F32) for _ in range(R)),
            tuple(jnp.zeros((1, QB), F32) for _ in range(R)),
            jnp.zeros((hd, R * QB), F32), jnp.ones((1, R * QB), F32))
    n_pairs = last // 2
    carry = lax.fori_loop(0, n_pairs, lambda i, c: pair_step(2 * i, c, False), init)
    _, l, acc, alpha = pair_step(2 * n_pairs, carry, True)
    acc = acc * alpha + dot(v_tile_t(2 * n_pairs + 1), pb_ref[...])
    o_slc = [acc[:, lanes(r)] * (1.0 / jnp.maximum(l[r], 1e-30)) for r in range(R)]

    outs = []
    for r in range(R):
        row = (g * R + r) * N_BRANCH
        gate = [jax.nn.sigmoid(gl_ref[0, pl.ds(row + br, 1), :]) for br in range(N_BRANCH)]
        outs.append(gate[0] * o_cmp[r] + gate[1] * o_slc[r] + gate[2] * o_win[r])
    o_ref[0] = jnp.concatenate(outs, axis=0).T.astype(o_ref.dtype)


def _nsa_attention(proj_t, shared, c2s_t):
    k_cmp, v_cmp_t, k_slc, v_slc_t, k_win, v_win_t = shared
    B, _, S = proj_t.shape
    G, R, hd, QB = NSA_GROUPS, NSA_REP, NSA_HEAD_DIM, Q_BLOCK
    n_cp = k_cmp.shape[2]
    n_slc = S // SLC_BLOCK
    gate_blk = (NSA_HEADS * hd) // LANES
    whole = lambda shape: pl.BlockSpec((1, 1) + shape, lambda b, g, q: (b, g) + (0,) * len(shape))
    return pl.pallas_call(
        _nsa_kernel,
        grid=(B, G, S // QB),
        in_specs=[pl.BlockSpec((1, R * hd, QB), lambda b, g, q: (b, g, q)),
                  pl.BlockSpec((1, LANES, QB), lambda b, g, q: (b, gate_blk, q)),
                  whole((n_cp, hd)), whole((hd, n_cp)),
                  whole((S, hd)), whole((S // SLC_TILE, hd, SLC_TILE)),
                  whole((S, hd)), whole((S // QB, hd, QB)),
                  pl.BlockSpec((n_slc, n_cp), lambda b, g, q: (0, 0))],
        out_specs=pl.BlockSpec((1, QB, R * hd), lambda b, g, q: (b, q, g)),
        out_shape=jax.ShapeDtypeStruct((B, S, NSA_HEADS * hd), BF16),
        scratch_shapes=[pltpu.VMEM((n_slc, QB), F32),
                        pltpu.VMEM((n_cp, R * QB), F32),
                        pltpu.VMEM((WIN_KEYS, R * QB), F32),
                        pltpu.VMEM((SLC_TILE, R * QB), F32),
                        pltpu.VMEM((SLC_TILE, R * QB), F32),
                        pltpu.VMEM((SLC_TILE, R * QB), BF16),
                        pltpu.VMEM((SLC_TILE, R * QB), BF16)],
        compiler_params=_params("parallel", "parallel", "arbitrary"),
        name="nsa_attention",
    )(proj_t, proj_t, k_cmp, v_cmp_t, k_slc, v_slc_t, k_win, v_win_t, c2s_t)


def _cmp_to_slc_t(n_cp, n_slc):
    i = np.arange(n_cp)[None, :] * CMP_STRIDE
    j = np.arange(n_slc)[:, None] * SLC_BLOCK
    ov = np.clip(np.minimum(i + CMP_LEN, j + SLC_BLOCK) - np.maximum(i, j), 0, None) / CMP_LEN
    ov[:, n_cp - 1] = 0.0
    return jnp.asarray(ov, dtype=BF16)


def _nsa_shared_kv(x, kv_sc, kv_sh, w_kv, cmp_pe, cmp_w1, cmp_b1, cmp_w2):
    B, S, _ = x.shape
    G, hd = NSA_GROUPS, NSA_HEAD_DIM
    kv = _modmm(x, kv_sc, kv_sh, w_kv.astype(BF16), transposed=False, tn=512)
    kv = kv.reshape(B, S, N_BRANCH, 2, G, hd)
    NP = S // CMP_STRIDE
    pieces = kv[:, :, 0].reshape(B, NP, CMP_STRIDE, 2, G, hd).transpose(3, 0, 4, 1, 2, 5)
    pieces = pieces.reshape(2, B, G, NP, CMP_STRIDE * hd)
    comp, comp_t = _compress(pieces, cmp_pe.reshape(2, 2, CMP_STRIDE * hd), cmp_w1.astype(BF16),
                             cmp_b1[:, None, :], cmp_w2.astype(BF16), cmp_w2.transpose(0, 2, 1).astype(BF16))
    to_k = lambda t: t.transpose(0, 2, 1, 3).astype(BF16)

    def to_vt(t, tile):
        return t.reshape(B, S // tile, tile, G, hd).transpose(0, 3, 1, 4, 2).astype(BF16)

    return (comp[0], comp_t[1], to_k(kv[:, :, 1, 0]), to_vt(kv[:, :, 1, 1], SLC_TILE),
            to_k(kv[:, :, 2, 0]), to_vt(kv[:, :, 2, 1], Q_BLOCK))


def kernel(x, c, ada_w, ada_b, ln_g, ln_b, ret_w_in, ret_w_o, kv_ada_w, kv_ada_b, nsa_w_kv, cmp_pe, cmp_w1, cmp_b1, cmp_w2, nsa_w_in, nsa_w_o, peer_w_q, peer_keys, peer_u, peer_v):
    B, S, D = x.shape
    mods = _cmod(c, ada_w, ada_b)
    kv_mod = _cmod(c, kv_ada_w[None], kv_ada_b[None])[0]
    ret_tables = _retention_tables(S, D // RET_HEADS)
    hd = NSA_HEAD_DIM
    n_gate = NSA_HEADS * N_BRANCH
    c2s_t = _cmp_to_slc_t(S // CMP_STRIDE, S // SLC_BLOCK)
    shared = None
    for layer in range(DEPTH):
        sh1, sc1, g1, sh2, sc2, g2 = [m[:, None, :] for m in jnp.split(mods[layer], 6, axis=-1)]
        if layer < N_A_LAYERS:
            proj = _modmm(x, sc1, sh1, ret_w_in[layer].astype(BF16), transposed=False, tn=512)
            a = _retention(proj, ret_tables)
            w_o = ret_w_o[layer]
        else:
            lb = layer - N_A_LAYERS
            w_in = nsa_w_in[lb]
            w_t = jnp.concatenate([w_in[:, :NSA_HEADS * hd].T * (hd ** -0.5), w_in[:, NSA_HEADS * hd:].T,
                                   jnp.zeros((LANES - n_gate, D), F32)], axis=0).astype(BF16)
            proj_t = _modmm(x, sc1, sh1, w_t, transposed=True, tn=384)
            a = _nsa_attention(proj_t, shared, c2s_t)
            w_o = nsa_w_o[lb]
        x = _oproj_ln(a, w_o.astype(BF16), x, g1, ln_g[layer, 0], ln_b[layer, 0])
        keys = peer_keys[layer].reshape(PEER_HEADS * 2, PEER_NKEYS, -1).astype(BF16)
        x = _peer(x, sc2, sh2, g2, peer_w_q[layer].T.astype(BF16), keys, peer_u[layer].astype(BF16),
                  peer_v[layer].T.astype(BF16), ln_g[layer, 1], ln_b[layer, 1])
        if layer == N_A_LAYERS - 1:
            kv_sh, kv_sc = [m[:, None, :] for m in jnp.split(kv_mod, 2, axis=-1)]
            shared = _nsa_shared_kv(x, kv_sc, kv_sh, nsa_w_kv, cmp_pe, cmp_w1, cmp_b1, cmp_w2)
    return x
```

```python
import functools

import numpy as np
import jax
import jax.numpy as jnp
from jax import lax
from jax.experimental import pallas as pl
from jax.experimental.pallas import tpu as pltpu

DEPTH = 4
N_A_LAYERS = DEPTH // 2
ALPHA = (2.0 * DEPTH) ** 0.25
LN_EPS = 1e-5
NEG_INF = -1e30
MASKED = 2.0 * NEG_INF
LOG2E = 1.4426950408889634

RET_HEADS = 4
RET_CHUNK = 128

NSA_HEADS = 16
NSA_GROUPS = 4
NSA_REP = NSA_HEADS // NSA_GROUPS
NSA_HEAD_DIM = 64
N_BRANCH = 3
CMP_STRIDE = 16
CMP_LEN = 32
SLC_BLOCK = 64
SLC_TOPK = 16
WINDOW = 512
Q_BLOCK = 128
FORCE_BONUS = 100.0
SLC_TILE = 512
WIN_KEYS = WINDOW + Q_BLOCK

PEER_HEADS = 8
PEER_NKEYS = 128
PEER_TOPK = 16

LANES = 128
VMEM_LIMIT = 56 * 1024 * 1024

F32 = jnp.float32
BF16 = jnp.bfloat16
_NT = (((1,), (1,)), ((), ()))


def _params(*sem):
    return pltpu.CompilerParams(dimension_semantics=sem, vmem_limit_bytes=VMEM_LIMIT)


def _split3(a):
    hi = a.astype(BF16)
    r1 = a - hi.astype(F32)
    mid = r1.astype(BF16)
    lo = (r1 - mid.astype(F32)).astype(BF16)
    return hi, mid, lo


def _layer_norm(z, g, b):
    mu = jnp.mean(z, axis=-1, keepdims=True)
    zc = z - mu
    var = jnp.mean(zc * zc, axis=-1, keepdims=True)
    return zc * lax.rsqrt(var + LN_EPS) * g + b


_GELU_K1 = -2.0 * 0.7978845608028654 * LOG2E
_GELU_K3 = _GELU_K1 * 0.044715


def _gelu_tanh(x):
    return x * (1.0 / (1.0 + jnp.exp2(x * (_GELU_K3 * (x * x) + _GELU_K1))))


def _cmod_kernel(c_ref, w_ref, b_ref, o_ref):
    c = c_ref[...]
    ca = c * jax.nn.sigmoid(c)
    w = w_ref[0]
    c_hi, c_mid, c_lo = _split3(ca)
    w_hi, w_mid, w_lo = _split3(w)
    dot = lambda a, b: jnp.dot(a, b, preferred_element_type=F32)
    acc = dot(c_hi, w_hi) + (dot(c_hi, w_mid) + dot(c_mid, w_hi)) + (dot(c_hi, w_lo) + dot(c_mid, w_mid) + dot(c_lo, w_hi))
    o_ref[0] = acc + b_ref[0]


def _cmod(c, w, b):
    L, D, N = w.shape
    n_b = c.shape[0]
    B = 16
    c = jnp.zeros((B, D), F32).at[:n_b].set(c)
    tn = 512
    out = pl.pallas_call(
        _cmod_kernel,
        grid=(L, N // tn),
        in_specs=[pl.BlockSpec((B, D), lambda l, n: (0, 0)),
                  pl.BlockSpec((1, D, tn), lambda l, n: (l, 0, n)),
                  pl.BlockSpec((1, 1, tn), lambda l, n: (l, 0, n))],
        out_specs=pl.BlockSpec((1, B, tn), lambda l, n: (l, 0, n)),
        out_shape=jax.ShapeDtypeStruct((L, B, N), F32),
        compiler_params=_params("parallel", "parallel"),
        name="cmod",
    )(c, w, b.reshape(L, 1, N))
    return out[:, :n_b]


def _modmm_kernel(x_ref, sc_ref, sh_ref, w_ref, o_ref, h_ref, *, transposed):
    @pl.when(pl.program_id(2) == 0)
    def _():
        h = x_ref[0] * (1.0 + sc_ref[0]) + sh_ref[0]
        h_ref[...] = h.astype(BF16)

    if transposed:
        o = lax.dot_general(w_ref[...], h_ref[...], _NT, preferred_element_type=F32)
    else:
        o = jnp.dot(h_ref[...], w_ref[...], preferred_element_type=F32)
    o_ref[0] = o.astype(o_ref.dtype)


def _modmm(x, sc, sh, w, *, transposed, tn, out_dtype=F32, tm=512):
    B, S, D = x.shape
    N = w.shape[0] if transposed else w.shape[1]
    if transposed:
        w_spec = pl.BlockSpec((tn, D), lambda b, s, n: (n, 0))
        o_spec = pl.BlockSpec((1, tn, tm), lambda b, s, n: (b, n, s))
        o_shape = (B, N, S)
    else:
        w_spec = pl.BlockSpec((D, tn), lambda b, s, n: (0, n))
        o_spec = pl.BlockSpec((1, tm, tn), lambda b, s, n: (b, s, n))
        o_shape = (B, S, N)
    vec = pl.BlockSpec((1, 1, D), lambda b, s, n: (b, 0, 0))
    return pl.pallas_call(
        functools.partial(_modmm_kernel, transposed=transposed),
        grid=(B, S // tm, N // tn),
        in_specs=[pl.BlockSpec((1, tm, D), lambda b, s, n: (b, s, 0)), vec, vec, w_spec],
        out_specs=o_spec,
        out_shape=jax.ShapeDtypeStruct(o_shape, out_dtype),
        scratch_shapes=[pltpu.VMEM((tm, D), BF16)],
        compiler_params=_params("parallel", "parallel", "arbitrary"),
        name="modmm_t" if transposed else "modmm",
    )(x, sc, sh, w)


def _ret_kernel(q_ref, k_ref, v_ref, g_ref, cos_ref, sin_ref, decay_ref, qdec_ref, kdec_ref, cdec_ref,
                o_ref, state_ref, *, dk):
    @pl.when(pl.program_id(2) == 0)
    def _():
        state_ref[...] = jnp.zeros_like(state_ref)

    cos = cos_ref[...]
    sin = sin_ref[...]
    half = dk // 2

    def rot(t):
        x1, x2 = t[:, :half], t[:, half:]
        return jnp.concatenate([x1 * cos - x2 * sin, x1 * sin + x2 * cos], axis=-1)

    q = rot(q_ref[0])
    k = rot(k_ref[0]) * (dk ** -0.5)
    vb = v_ref[0].astype(BF16)
    qb = q.astype(BF16)
    s = lax.dot_general(qb, k.astype(BF16), _NT, preferred_element_type=F32) * decay_ref[0]
    inner = jnp.dot(s.astype(BF16), vb, preferred_element_type=F32)
    state = state_ref[...]
    cross = jnp.dot(qb, state.astype(BF16), preferred_element_type=F32) * qdec_ref[0]
    kd_t = (k * kdec_ref[0]).T.astype(BF16)
    state_ref[...] = state * cdec_ref[0] + jnp.dot(kd_t, vb, preferred_element_type=F32)
    o = inner + cross
    mu = jnp.mean(o, axis=-1, keepdims=True)
    oc = o - mu
    var = jnp.mean(oc * oc, axis=-1, keepdims=True)
    on = oc * lax.rsqrt(var + LN_EPS)
    g = g_ref[0]
    o_ref[0] = (g * jax.nn.sigmoid(g) * on).astype(o_ref.dtype)


def _retention(proj, tables):
    B, S, n_in = proj.shape
    H, C = RET_HEADS, RET_CHUNK
    dk = n_in // (6 * H)
    dv = 2 * dk
    cos, sin, decay, qdec, kdec, cdec = tables
    return pl.pallas_call(
        functools.partial(_ret_kernel, dk=dk),
        grid=(B, H, S // C),
        in_specs=[
            pl.BlockSpec((1, C, dk), lambda b, h, c: (b, c, h)),
            pl.BlockSpec((1, C, dk), lambda b, h, c: (b, c, H + h)),
            pl.BlockSpec((1, C, dv), lambda b, h, c: (b, c, H + h)),
            pl.BlockSpec((1, C, dv), lambda b, h, c: (b, c, 2 * H + h)),
            pl.BlockSpec((C, dk // 2), lambda b, h, c: (c, 0)),
            pl.BlockSpec((C, dk // 2), lambda b, h, c: (c, 0)),
            pl.BlockSpec((1, C, C), lambda b, h, c: (h, 0, 0)),
            pl.BlockSpec((1, C, dv), lambda b, h, c: (h, 0, 0)),
            pl.BlockSpec((1, C, dk), lambda b, h, c: (h, 0, 0)),
            pl.BlockSpec((1, 1, dv), lambda b, h, c: (h, 0, 0)),
        ],
        out_specs=pl.BlockSpec((1, C, dv), lambda b, h, c: (b, c, h)),
        out_shape=jax.ShapeDtypeStruct((B, S, H * dv), BF16),
        scratch_shapes=[pltpu.VMEM((dk, dv), F32)],
        compiler_params=_params("parallel", "parallel", "arbitrary"),
        name="retention",
    )(proj, proj, proj, proj, cos, sin, decay, qdec, kdec, cdec)


def _retention_tables(S, dk):
    H, C = RET_HEADS, RET_CHUNK
    dv = 2 * dk
    pos = jnp.arange(S, dtype=F32)
    theta = 1.0 / (10000.0 ** jnp.linspace(0.0, 1.0, dk // 2, dtype=F32))
    ang = pos[:, None] * theta[None, :]
    log_g = jnp.log1p(-jnp.exp2(-5.0 - jnp.arange(H, dtype=F32)))
    idx = jnp.arange(C, dtype=F32)
    diff = idx[:, None] - idx[None, :]
    decay = jnp.where(diff >= 0, jnp.exp(jnp.maximum(diff, 0.0)[None] * log_g[:, None, None]), 0.0)
    q_dec = jnp.exp((idx + 1.0)[None] * log_g[:, None])
    k_dec = jnp.exp((C - 1.0 - idx)[None] * log_g[:, None])
    c_dec = jnp.exp(C * log_g)
    return (jnp.cos(ang), jnp.sin(ang), decay,
            jnp.broadcast_to(q_dec[:, :, None], (H, C, dv)),
            jnp.broadcast_to(k_dec[:, :, None], (H, C, dk)),
            jnp.broadcast_to(c_dec[:, None, None], (H, 1, dv)))


def _oproj_ln_kernel(a_ref, w_ref, x_ref, g_ref, lg_ref, lb_ref, o_ref):
    y = jnp.dot(a_ref[0], w_ref[...], preferred_element_type=F32)
    z = ALPHA * x_ref[0] + g_ref[0] * y
    o_ref[0] = _layer_norm(z, lg_ref[...], lb_ref[...])


def _oproj_ln(a, w, x, gate, lg, lb, tm=512):
    B, S, D = x.shape
    K = a.shape[-1]
    vec = pl.BlockSpec((1, 1, D), lambda b, s: (b, 0, 0))
    par = pl.BlockSpec((1, D), lambda b, s: (0, 0))
    return pl.pallas_call(
        _oproj_ln_kernel,
        grid=(B, S // tm),
        in_specs=[pl.BlockSpec((1, tm, K), lambda b, s: (b, s, 0)),
                  pl.BlockSpec((K, D), lambda b, s: (0, 0)),
                  pl.BlockSpec((1, tm, D), lambda b, s: (b, s, 0)),
                  vec, par, par],
        out_specs=pl.BlockSpec((1, tm, D), lambda b, s: (b, s, 0)),
        out_shape=jax.ShapeDtypeStruct((B, S, D), F32),
        compiler_params=_params("parallel", "parallel"),
        name="oproj_ln",
    )(a, w, x, gate, lg.reshape(1, D), lb.reshape(1, D))


def _top_values(s, k):
    vals = []
    for _ in range(k):
        m = jnp.max(s, axis=0, keepdims=True)
        vals.append(m)
        s = jnp.where(s == m, -jnp.inf, s)
    return vals


def _top_values_ranked(s, k):
    vals = []
    rank = jnp.full(s.shape, float(k), F32)
    for i in range(k):
        m = jnp.max(s, axis=0, keepdims=True)
        vals.append(m)
        hit = s == m
        rank = jnp.where(hit, float(i), rank)
        s = jnp.where(hit, -jnp.inf, s)
    return vals, rank


def _twice_bf16(x):
    b = pltpu.bitcast(x, jnp.uint32)
    return b | (b >> 16)


_PEER_PAIRS = [(p, q) for p in range(PEER_TOPK) for q in range(PEER_TOPK) if (p + 1) * (q + 1) <= PEER_TOPK + 1]
_PEER_CAND_ROWS = -(-len(_PEER_PAIRS) // 8) * 8


def _peer_kernel(x_ref, sc_ref, sh_ref, g_ref, wq_ref, keys_ref, u_ref, vt_ref, lg_ref, lb_ref, o_ref,
                 h_ref, w1w_ref, lw_ref, w2b_ref, r2b_ref, cand_ref, a_ref, ws_ref, p_ref, acc_ref, *, te, sub, kchunk, wb):
    e = pl.program_id(2)
    tm = h_ref.shape[0]

    @pl.when(e == 0)
    def _route():
        h = (x_ref[0] * (1.0 + sc_ref[0]) + sh_ref[0]).astype(BF16)
        h_ref[...] = h
        q_t = lax.dot_general(wq_ref[...], h, _NT, preferred_element_type=F32).astype(BF16)
        dq = keys_ref.shape[2]
        cand_ref[...] = jnp.full(cand_ref.shape, -jnp.inf, F32)
        for hd in range(PEER_HEADS):
            s1_all = jnp.dot(keys_ref[2 * hd], q_t[(2 * hd) * dq:(2 * hd + 1) * dq], preferred_element_type=F32)
            s2_all = jnp.dot(keys_ref[2 * hd + 1], q_t[(2 * hd + 1) * dq:(2 * hd + 2) * dq], preferred_element_type=F32)
            for lt in range(tm // LANES):
                ln = slice(lt * LANES, (lt + 1) * LANES)
                s1, s2 = s1_all[:, ln], s2_all[:, ln]
                a1 = _top_values(s1, PEER_TOPK)
                a2, rank2 = _top_values_ranked(s2, PEER_TOPK)
                for i, (p, q) in enumerate(_PEER_PAIRS):
                    cand_ref[i:i + 1, ln] = a1[p] + a2[q]
                c = _top_values(cand_ref[:, ln], PEER_TOPK + 1)
                z = jnp.ones_like(c[0])
                for kk in range(1, PEER_TOPK):
                    z = z + jnp.exp(c[kk] - c[0])
                inv_z = 1.0 / z
                tau = 0.5 * (c[PEER_TOPK - 1] + c[PEER_TOPK])
                n_ok = jnp.zeros_like(s1)
                for q in range(PEER_TOPK):
                    n_ok = jnp.where(s1 >= tau - a2[q], float(q + 1), n_ok)
                w1 = (jnp.exp(s1 - a1[0]) * inv_z).astype(BF16).astype(F32)
                by_step = lambda t: t.reshape(w1w_ref.shape[1], w1w_ref.shape[2], LANES)
                w1w_ref[hd, :, :, ln] = by_step(_twice_bf16(w1))
                lw_ref[hd, :, :, ln] = by_step(_twice_bf16(n_ok))
                w2b_ref[hd, :, ln] = jnp.exp(s2 - a2[0]).astype(BF16)
                r2b_ref[hd, :, ln] = rank2.astype(BF16)
        acc_ref[...] = jnp.zeros_like(acc_ref)

    per_sub = sub // PEER_NKEYS
    n_sub = te // sub

    def activations(sb):
        a_ref[sb] = lax.dot_general(u_ref[sb * sub:(sb + 1) * sub, :], h_ref[...], _NT,
                                    preferred_element_type=F32)

    activations(0)
    for sb in range(n_sub):
        for ii in range(per_sub):
            k = sb * per_sub + ii
            for lt in range(tm // wb):
                ln = slice(lt * wb, (lt + 1) * wb)
                wsum = None
                for hd in range(PEER_HEADS):
                    row_bf16 = lambda ref: pltpu.bitcast(
                        jnp.broadcast_to(ref[hd, e, k:k + 1, ln], (PEER_NKEYS // 2, wb)), BF16)
                    keep = r2b_ref[hd, :, ln] < row_bf16(lw_ref)
                    term = jnp.where(keep, w2b_ref[hd, :, ln], jnp.zeros((), BF16)) * row_bf16(w1w_ref)
                    wsum = term if wsum is None else wsum + term
                ws_ref[ii * PEER_NKEYS:(ii + 1) * PEER_NKEYS, ln] = wsum
        if sb + 1 < n_sub:
            activations(sb + 1)
        for ii in range(per_sub):
            rows = slice(ii * PEER_NKEYS, (ii + 1) * PEER_NKEYS)
            p_ref[pl.ds(sb * sub + ii * PEER_NKEYS, PEER_NKEYS), :] = (
                _gelu_tanh(a_ref[sb, rows, :]).astype(BF16) * ws_ref[rows, :])
        done = (sb + 1) * sub
        if done % kchunk == 0:
            cols = slice(done - kchunk, done)
            acc_ref[...] += jnp.dot(vt_ref[:, cols], p_ref[cols, :], preferred_element_type=F32)

    @pl.when(e == pl.num_programs(2) - 1)
    def _finish():
        y = acc_ref[...].T
        z = ALPHA * x_ref[0] + g_ref[0] * y
        o_ref[0] = _layer_norm(z, lg_ref[...], lb_ref[...])


def _peer(x, sc, sh, gate, wq_t, keys, u, v_t, lg, lb, tm=512, te=1024, sub=256, kchunk=512, wb=256):
    B, S, D = x.shape
    E = u.shape[0]
    nq = wq_t.shape[0]
    vec = pl.BlockSpec((1, 1, D), lambda b, s, e: (b, 0, 0))
    par = pl.BlockSpec((1, D), lambda b, s, e: (0, 0))
    return pl.pallas_call(
        functools.partial(_peer_kernel, te=te, sub=sub, kchunk=kchunk, wb=wb),
        grid=(B, S // tm, E // te),
        in_specs=[pl.BlockSpec((1, tm, D), lambda b, s, e: (b, s, 0)), vec, vec, vec,
                  pl.BlockSpec((nq, D), lambda b, s, e: (0, 0)),
                  pl.BlockSpec(keys.shape, lambda b, s, e: (0, 0, 0)),
                  pl.BlockSpec((te, D), lambda b, s, e: (e, 0)),
                  pl.BlockSpec((D, te), lambda b, s, e: (0, e)),
                  par, par],
        out_specs=pl.BlockSpec((1, tm, D), lambda b, s, e: (b, s, 0)),
        out_shape=jax.ShapeDtypeStruct((B, S, D), F32),
        scratch_shapes=[pltpu.VMEM((tm, D), BF16),
                        pltpu.VMEM((PEER_HEADS, E // te, te // PEER_NKEYS, tm), jnp.uint32),
                        pltpu.VMEM((PEER_HEADS, E // te, te // PEER_NKEYS, tm), jnp.uint32),
                        pltpu.VMEM((PEER_HEADS, PEER_NKEYS, tm), BF16),
                        pltpu.VMEM((PEER_HEADS, PEER_NKEYS, tm), BF16),
                        pltpu.VMEM((_PEER_CAND_ROWS, tm), F32),
                        pltpu.VMEM((te // sub, sub, tm), F32),
                        pltpu.VMEM((sub, tm), BF16),
                        pltpu.VMEM((te, tm), BF16),
                        pltpu.VMEM((D, tm), F32)],
        compiler_params=_params("parallel", "parallel", "arbitrary"),
        name="peer",
    )(x, sc, sh, gate, wq_t, keys, u, v_t, lg.reshape(1, D), lb.reshape(1, D))


def _compress_kernel(pc_ref, pe_ref, w1_ref, b1_ref, w2_ref, w2t_ref, o_ref, ot_ref):
    pc = pc_ref[0, 0, 0]
    half = pc.shape[1]
    lo = (pc + pe_ref[0, 0:1, :]).astype(BF16)
    hi = (pc + pe_ref[0, 1:2, :]).astype(BF16)
    a = jnp.dot(lo, w1_ref[0, :half, :], preferred_element_type=F32)
    b = jnp.dot(hi, w1_ref[0, half:, :], preferred_element_type=F32)
    b_next = pltpu.roll(b, pc.shape[0] - 1, 0)
    hid = jax.nn.gelu(a + b_next + b1_ref[0]).astype(BF16)
    o_ref[0, 0, 0] = jnp.dot(hid, w2_ref[0], preferred_element_type=F32).astype(o_ref.dtype)
    ot_ref[0, 0, 0] = lax.dot_general(w2t_ref[0], hid, _NT, preferred_element_type=F32).astype(ot_ref.dtype)


def _compress(pieces, pe2, w1, b1, w2, w2t):
    _, B, G, NP, F = pieces.shape
    Hd = w1.shape[2]
    hd = w2.shape[2]
    return pl.pallas_call(
        _compress_kernel,
        grid=(2, B, G),
        in_specs=[pl.BlockSpec((1, 1, 1, NP, F), lambda c, b, g: (c, b, g, 0, 0)),
                  pl.BlockSpec((1, 2, F), lambda c, b, g: (c, 0, 0)),
                  pl.BlockSpec((1, 2 * F, Hd), lambda c, b, g: (c, 0, 0)),
                  pl.BlockSpec((1, 1, Hd), lambda c, b, g: (c, 0, 0)),
                  pl.BlockSpec((1, Hd, hd), lambda c, b, g: (c, 0, 0)),
                  pl.BlockSpec((1, hd, Hd), lambda c, b, g: (c, 0, 0))],
        out_specs=[pl.BlockSpec((1, 1, 1, NP, hd), lambda c, b, g: (c, b, g, 0, 0)),
                   pl.BlockSpec((1, 1, 1, hd, NP), lambda c, b, g: (c, b, g, 0, 0))],
        out_shape=[jax.ShapeDtypeStruct((2, B, G, NP, hd), BF16),
                   jax.ShapeDtypeStruct((2, B, G, hd, NP), BF16)],
        compiler_params=_params("parallel", "parallel", "parallel"),
        name="compress",
    )(pieces, pe2, w1, b1, w2, w2t)


def _softmax_t(s, mask):
    sm = jnp.where(mask, s, MASKED)
    m = jnp.maximum(jnp.max(sm, axis=0, keepdims=True), NEG_INF)
    e = jnp.exp2(sm - m)
    return e, 1.0 / jnp.maximum(jnp.sum(e, axis=0, keepdims=True), 1e-30)


def _nsa_kernel(qt_ref, gl_ref, kc_ref, vct_ref, ks_ref, vst_ref, kw_ref, vwt_ref, c2s_ref, o_ref,
                sel_ref, sc_ref, sw_ref, sa_ref, sb_ref, pa_ref, pb_ref):
    R, hd, QB = NSA_REP, NSA_HEAD_DIM, Q_BLOCK
    g = pl.program_id(1)
    start = pl.program_id(2) * QB
    q4 = qt_ref[0] * LOG2E
    q_t = jnp.concatenate([q4[r * hd:(r + 1) * hd] for r in range(R)], axis=1).astype(BF16)
    t_q = start + lax.broadcasted_iota(jnp.int32, (1, QB), 1)
    lanes = lambda r: slice(r * QB, (r + 1) * QB)
    dot = lambda a, b: jnp.dot(a, b, preferred_element_type=F32)
    n_tiles = ks_ref.shape[2] // SLC_TILE

    def k_slc_tile(kt):
        return ks_ref[0, 0, pl.ds(pl.multiple_of(kt * SLC_TILE, SLC_TILE), SLC_TILE), :]

    base = pl.multiple_of(jnp.maximum(start - WINDOW, 0), QB)
    sc_ref[...] = dot(kc_ref[0, 0], q_t)
    sw_ref[...] = dot(kw_ref[0, 0, pl.ds(base, WIN_KEYS), :], q_t)
    sa_ref[...] = dot(k_slc_tile(0), q_t)

    n_cp = kc_ref.shape[2]
    n_idx = lax.broadcasted_iota(jnp.int32, (n_cp, 1), 0)
    cmask = (n_idx * CMP_STRIDE + (CMP_LEN - 1)) <= t_q
    v_cmp_t = vct_ref[0, 0]
    o_cmp, psum = [], None
    for r in range(R):
        e, inv = _softmax_t(sc_ref[:, lanes(r)], cmask)
        p = e * inv
        o_cmp.append(dot(v_cmp_t, p.astype(BF16)))
        psum = p if psum is None else psum + p

    kp = base + lax.broadcasted_iota(jnp.int32, (WIN_KEYS, 1), 0)
    wmask = (kp <= t_q) & (kp > t_q - WINDOW)
    o_win = []
    for r in range(R):
        e, inv = _softmax_t(sw_ref[:, lanes(r)], wmask)
        e = e.astype(BF16)
        acc = None
        for a in range(WIN_KEYS // QB):
            d = dot(vwt_ref[0, 0, base // QB + a], e[a * QB:(a + 1) * QB])
            acc = d if acc is None else acc + d
        o_win.append(acc * inv)

    c2s = c2s_ref[...]
    imp = None
    for part in _split3(psum):
        d = dot(c2s, part)
        imp = d if imp is None else imp + d
    n_slc = c2s.shape[0]
    j_idx = lax.broadcasted_iota(jnp.int32, (n_slc, 1), 0)
    cur = jnp.right_shift(t_q, SLC_BLOCK.bit_length() - 1)
    forced = (j_idx == 0) | (j_idx == cur) | (j_idx == cur - 1)
    avail = (j_idx * SLC_BLOCK) <= t_q
    score = jnp.where(avail, imp + jnp.where(forced, FORCE_BONUS, 0.0), -1.0)
    work = score
    sel = jnp.zeros(score.shape, F32)
    for _ in range(min(SLC_TOPK, n_slc)):
        mx = jnp.max(work, axis=0, keepdims=True)
        first = jnp.min(jnp.where(work == mx, j_idx, n_slc), axis=0, keepdims=True)
        hit = j_idx == first
        sel = jnp.where(hit, 1.0, sel)
        work = jnp.where(hit, -jnp.inf, work)
    sel_ref[...] = jnp.where(score >= 0.0, sel, 0.0)

    per_tile = SLC_TILE // SLC_BLOCK
    row_tok = lax.broadcasted_iota(jnp.int32, (SLC_TILE, 1), 0)
    last = start // SLC_TILE

    def tile_mask(kt, causal):
        blk = pl.multiple_of(jnp.minimum(kt, n_tiles - 1) * per_tile, per_tile)
        sel_rows = sel_ref[pl.ds(blk, per_tile), :]
        mask = jnp.concatenate(
            [jnp.broadcast_to(sel_rows[a:a + 1, :], (SLC_BLOCK, QB)) for a in range(per_tile)], axis=0) > 0.5
        if causal:
            mask = mask & ((kt * SLC_TILE + row_tok) <= t_q)
        return mask

    def softmax_update(s_buf, p_buf, mask, m, l):
        m_out, l_out, alphas = [], [], []
        for r in range(R):
            sm = jnp.where(mask, s_buf[:, lanes(r)], MASKED)
            m_new = jnp.maximum(m[r], jnp.max(sm, axis=0, keepdims=True))
            p = jnp.exp2(sm - m_new)
            alphas.append(jnp.exp2(m[r] - m_new))
            p_buf[:, lanes(r)] = p.astype(BF16)
            m_out.append(m_new)
            l_out.append(l[r] * alphas[r] + jnp.sum(p, axis=0, keepdims=True))
        return tuple(m_out), tuple(l_out), jnp.concatenate(alphas, axis=1)

    v_tile_t = lambda kt: vst_ref[0, 0, jnp.clip(kt, 0, n_tiles - 1)]

    def pair_step(t0, carry, causal):
        m, l, acc, alpha_prev = carry
        sb_ref[...] = dot(k_slc_tile(jnp.minimum(t0 + 1, n_tiles - 1)), q_t)
        m, l, alpha0 = softmax_update(sa_ref, pa_ref, tile_mask(t0, causal), m, l)
        acc = acc * alpha_prev + dot(v_tile_t(t0 - 1), pb_ref[...])
        sa_ref[...] = dot(k_slc_tile(jnp.minimum(t0 + 2, n_tiles - 1)), q_t)
        m, l, alpha1 = softmax_update(sb_ref, pb_ref, tile_mask(t0 + 1, causal), m, l)
        acc = acc * alpha0 + dot(v_tile_t(t0), pa_ref[...])
        return m, l, acc, alpha1

    pb_ref[...] = jnp.zeros_like(pb_ref)
    init = (tuple(jnp.full((1, QB), NEG_INF, F32) for _ in range(R)),
            tuple(jnp.zeros((1, QB), F32) for _ in range(R)),
            jnp.zeros((hd, R * QB), F32), jnp.ones((1, R * QB), F32))
    n_pairs = last // 2
    carry = lax.fori_loop(0, n_pairs, lambda i, c: pair_step(2 * i, c, False), init)
    _, l, acc, alpha = pair_step(2 * n_pairs, carry, True)
    acc = acc * alpha + dot(v_tile_t(2 * n_pairs + 1), pb_ref[...])
    o_slc = [acc[:, lanes(r)] * (1.0 / jnp.maximum(l[r], 1e-30)) for r in range(R)]

    outs = []
    for r in range(R):
        row = (g * R + r) * N_BRANCH
        gate = [jax.nn.sigmoid(gl_ref[0, pl.ds(row + br, 1), :]) for br in range(N_BRANCH)]
        outs.append(gate[0] * o_cmp[r] + gate[1] * o_slc[r] + gate[2] * o_win[r])
    o_ref[0] = jnp.concatenate(outs, axis=0).T.astype(o_ref.dtype)


def _nsa_attention(proj_t, shared, c2s_t):
    k_cmp, v_cmp_t, k_slc, v_slc_t, k_win, v_win_t = shared
    B, _, S = proj_t.shape
    G, R, hd, QB = NSA_GROUPS, NSA_REP, NSA_HEAD_DIM, Q_BLOCK
    n_cp = k_cmp.shape[2]
    n_slc = S // SLC_BLOCK
    gate_blk = (NSA_HEADS * hd) // LANES
    whole = lambda shape: pl.BlockSpec((1, 1) + shape, lambda b, g, q: (b, g) + (0,) * len(shape))
    return pl.pallas_call(
        _nsa_kernel,
        grid=(B, G, S // QB),
        in_specs=[pl.BlockSpec((1, R * hd, QB), lambda b, g, q: (b, g, q)),
                  pl.BlockSpec((1, LANES, QB), lambda b, g, q: (b, gate_blk, q)),
                  whole((n_cp, hd)), whole((hd, n_cp)),
                  whole((S, hd)), whole((S // SLC_TILE, hd, SLC_TILE)),
                  whole((S, hd)), whole((S // QB, hd, QB)),
                  pl.BlockSpec((n_slc, n_cp), lambda b, g, q: (0, 0))],
        out_specs=pl.BlockSpec((1, QB, R * hd), lambda b, g, q: (b, q, g)),
        out_shape=jax.ShapeDtypeStruct((B, S, NSA_HEADS * hd), BF16),
        scratch_shapes=[pltpu.VMEM((n_slc, QB), F32),
                        pltpu.VMEM((n_cp, R * QB), F32),
                        pltpu.VMEM((WIN_KEYS, R * QB), F32),
                        pltpu.VMEM((SLC_TILE, R * QB), F32),
                        pltpu.VMEM((SLC_TILE, R * QB), F32),
                        pltpu.VMEM((SLC_TILE, R * QB), BF16),
                        pltpu.VMEM((SLC_TILE, R * QB), BF16)],
        compiler_params=_params("parallel", "parallel", "arbitrary"),
        name="nsa_attention",
    )(proj_t, proj_t, k_cmp, v_cmp_t, k_slc, v_slc_t, k_win, v_win_t, c2s_t)


def _cmp_to_slc_t(n_cp, n_slc):
    i = np.arange(n_cp)[None, :] * CMP_STRIDE
    j = np.arange(n_slc)[:, None] * SLC_BLOCK
    ov = np.clip(np.minimum(i + CMP_LEN, j + SLC_BLOCK) - np.maximum(i, j), 0, None) / CMP_LEN
    ov[:, n_cp - 1] = 0.0
    return jnp.asarray(ov, dtype=BF16)


def _nsa_shared_kv(x, kv_sc, kv_sh, w_kv, cmp_pe, cmp_w1, cmp_b1, cmp_w2):
    B, S, _ = x.shape
    G, hd = NSA_GROUPS, NSA_HEAD_DIM
    kv = _modmm(x, kv_sc, kv_sh, w_kv.astype(BF16), transposed=False, tn=512)
    kv = kv.reshape(B, S, N_BRANCH, 2, G, hd)
    NP = S // CMP_STRIDE
    pieces = kv[:, :, 0].reshape(B, NP, CMP_STRIDE, 2, G, hd).transpose(3, 0, 4, 1, 2, 5)
    pieces = pieces.reshape(2, B, G, NP, CMP_STRIDE * hd)
    comp, comp_t = _compress(pieces, cmp_pe.reshape(2, 2, CMP_STRIDE * hd), cmp_w1.astype(BF16),
                             cmp_b1[:, None, :], cmp_w2.astype(BF16), cmp_w2.transpose(0, 2, 1).astype(BF16))
    to_k = lambda t: t.transpose(0, 2, 1, 3).astype(BF16)

    def to_vt(t, tile):
        return t.reshape(B, S // tile, tile, G, hd).transpose(0, 3, 1, 4, 2).astype(BF16)

    return (comp[0], comp_t[1], to_k(kv[:, :, 1, 0]), to_vt(kv[:, :, 1, 1], SLC_TILE),
            to_k(kv[:, :, 2, 0]), to_vt(kv[:, :, 2, 1], Q_BLOCK))


def kernel(x, c, ada_w, ada_b, ln_g, ln_b, ret_w_in, ret_w_o, kv_ada_w, kv_ada_b, nsa_w_kv, cmp_pe, cmp_w1, cmp_b1, cmp_w2, nsa_w_in, nsa_w_o, peer_w_q, peer_keys, peer_u, peer_v):
    B, S, D = x.shape
    mods = _cmod(c, ada_w, ada_b)
    kv_mod = _cmod(c, kv_ada_w[None], kv_ada_b[None])[0]
    ret_tables = _retention_tables(S, D // RET_HEADS)
    hd = NSA_HEAD_DIM
    n_gate = NSA_HEADS * N_BRANCH
    c2s_t = _cmp_to_slc_t(S // CMP_STRIDE, S // SLC_BLOCK)
    shared = None
    for layer in range(DEPTH):
        sh1, sc1, g1, sh2, sc2, g2 = [m[:, None, :] for m in jnp.split(mods[layer], 6, axis=-1)]
        if layer < N_A_LAYERS:
            proj = _modmm(x, sc1, sh1, ret_w_in[layer].astype(BF16), transposed=False, tn=512)
            a = _retention(proj, ret_tables)
            w_o = ret_w_o[layer]
        else:
            lb = layer - N_A_LAYERS
            w_in = nsa_w_in[lb]
            w_t = jnp.concatenate([w_in[:, :NSA_HEADS * hd].T * (hd ** -0.5), w_in[:, NSA_HEADS * hd:].T,
                                   jnp.zeros((LANES - n_gate, D), F32)], axis=0).astype(BF16)
            proj_t = _modmm(x, sc1, sh1, w_t, transposed=True, tn=384)
            a = _nsa_attention(proj_t, shared, c2s_t)
            w_o = nsa_w_o[lb]
        x = _oproj_ln(a, w_o.astype(BF16), x, g1, ln_g[layer, 0], ln_b[layer, 0])
        keys = peer_keys[layer].reshape(PEER_HEADS * 2, PEER_NKEYS, -1).astype(BF16)
        x = _peer(x, sc2, sh2, g2, peer_w_q[layer].T.astype(BF16), keys, peer_u[layer].astype(BF16),
                  peer_v[layer].T.astype(BF16), ln_g[layer, 1], ln_b[layer, 1])
        if layer == N_A_LAYERS - 1:
            kv_sh, kv_sc = [m[:, None, :] for m in jnp.split(kv_mod, 2, axis=-1)]
            shared = _nsa_shared_kv(x, kv_sc, kv_sh, nsa_w_kv, cmp_pe, cmp_w1, cmp_b1, cmp_w2)
    return x
```

```python
import functools

import numpy as np
import jax
import jax.numpy as jnp
from jax import lax
from jax.experimental import pallas as pl
from jax.experimental.pallas import tpu as pltpu

DEPTH = 4
N_A_LAYERS = DEPTH // 2
ALPHA = (2.0 * DEPTH) ** 0.25
LN_EPS = 1e-5
NEG_INF = -1e30
MASKED = 2.0 * NEG_INF
LOG2E = 1.4426950408889634

RET_HEADS = 4
RET_CHUNK = 128

NSA_HEADS = 16
NSA_GROUPS = 4
NSA_REP = NSA_HEADS // NSA_GROUPS
NSA_HEAD_DIM = 64
N_BRANCH = 3
CMP_STRIDE = 16
CMP_LEN = 32
SLC_BLOCK = 64
SLC_TOPK = 16
WINDOW = 512
Q_BLOCK = 128
FORCE_BONUS = 100.0
SLC_TILE = 512
WIN_KEYS = WINDOW + Q_BLOCK

PEER_HEADS = 8
PEER_NKEYS = 128
PEER_TOPK = 16

LANES = 128
VMEM_LIMIT = 56 * 1024 * 1024

F32 = jnp.float32
BF16 = jnp.bfloat16
_NT = (((1,), (1,)), ((), ()))


def _params(*sem):
    return pltpu.CompilerParams(dimension_semantics=sem, vmem_limit_bytes=VMEM_LIMIT)


def _split3(a):
    hi = a.astype(BF16)
    r1 = a - hi.astype(F32)
    mid = r1.astype(BF16)
    lo = (r1 - mid.astype(F32)).astype(BF16)
    return hi, mid, lo


def _layer_norm(z, g, b):
    mu = jnp.mean(z, axis=-1, keepdims=True)
    zc = z - mu
    var = jnp.mean(zc * zc, axis=-1, keepdims=True)
    return zc * lax.rsqrt(var + LN_EPS) * g + b


_GELU_K1 = -2.0 * 0.7978845608028654 * LOG2E
_GELU_K3 = _GELU_K1 * 0.044715


def _gelu_tanh(x):
    return x * (1.0 / (1.0 + jnp.exp2(x * (_GELU_K3 * (x * x) + _GELU_K1))))


def _cmod_kernel(c_ref, w_ref, b_ref, o_ref):
    c = c_ref[...]
    ca = c * jax.nn.sigmoid(c)
    w = w_ref[0]
    c_hi, c_mid, c_lo = _split3(ca)
    w_hi, w_mid, w_lo = _split3(w)
    dot = lambda a, b: jnp.dot(a, b, preferred_element_type=F32)
    acc = dot(c_hi, w_hi) + (dot(c_hi, w_mid) + dot(c_mid, w_hi)) + (dot(c_hi, w_lo) + dot(c_mid, w_mid) + dot(c_lo, w_hi))
    o_ref[0] = acc + b_ref[0]


def _cmod(c, w, b):
    L, D, N = w.shape
    n_b = c.shape[0]
    B = 16
    c = jnp.zeros((B, D), F32).at[:n_b].set(c)
    tn = 512
    out = pl.pallas_call(
        _cmod_kernel,
        grid=(L, N // tn),
        in_specs=[pl.BlockSpec((B, D), lambda l, n: (0, 0)),
                  pl.BlockSpec((1, D, tn), lambda l, n: (l, 0, n)),
                  pl.BlockSpec((1, 1, tn), lambda l, n: (l, 0, n))],
        out_specs=pl.BlockSpec((1, B, tn), lambda l, n: (l, 0, n)),
        out_shape=jax.ShapeDtypeStruct((L, B, N), F32),
        compiler_params=_params("parallel", "parallel"),
        name="cmod",
    )(c, w, b.reshape(L, 1, N))
    return out[:, :n_b]


def _modmm_kernel(x_ref, sc_ref, sh_ref, w_ref, o_ref, h_ref, *, transposed):
    @pl.when(pl.program_id(2) == 0)
    def _():
        h = x_ref[0] * (1.0 + sc_ref[0]) + sh_ref[0]
        h_ref[...] = h.astype(BF16)

    if transposed:
        o = lax.dot_general(w_ref[...], h_ref[...], _NT, preferred_element_type=F32)
    else:
        o = jnp.dot(h_ref[...], w_ref[...], preferred_element_type=F32)
    o_ref[0] = o.astype(o_ref.dtype)


def _modmm(x, sc, sh, w, *, transposed, tn, out_dtype=F32, tm=512):
    B, S, D = x.shape
    N = w.shape[0] if transposed else w.shape[1]
    if transposed:
        w_spec = pl.BlockSpec((tn, D), lambda b, s, n: (n, 0))
        o_spec = pl.BlockSpec((1, tn, tm), lambda b, s, n: (b, n, s))
        o_shape = (B, N, S)
    else:
        w_spec = pl.BlockSpec((D, tn), lambda b, s, n: (0, n))
        o_spec = pl.BlockSpec((1, tm, tn), lambda b, s, n: (b, s, n))
        o_shape = (B, S, N)
    vec = pl.BlockSpec((1, 1, D), lambda b, s, n: (b, 0, 0))
    return pl.pallas_call(
        functools.partial(_modmm_kernel, transposed=transposed),
        grid=(B, S // tm, N // tn),
        in_specs=[pl.BlockSpec((1, tm, D), lambda b, s, n: (b, s, 0)), vec, vec, w_spec],
        out_specs=o_spec,
        out_shape=jax.ShapeDtypeStruct(o_shape, out_dtype),
        scratch_shapes=[pltpu.VMEM((tm, D), BF16)],
        compiler_params=_params("parallel", "parallel", "arbitrary"),
        name="modmm_t" if transposed else "modmm",
    )(x, sc, sh, w)


def _ret_kernel(q_ref, k_ref, v_ref, g_ref, cos_ref, sin_ref, decay_ref, qdec_ref, kdec_ref, cdec_ref,
                o_ref, state_ref, *, dk, cps):
    @pl.when(pl.program_id(2) == 0)
    def _():
        state_ref[...] = jnp.zeros_like(state_ref)

    C = RET_CHUNK
    half = dk // 2
    state = state_ref[...]
    for j in range(cps):
        rows = slice(j * C, (j + 1) * C)
        cos, sin = cos_ref[rows, :], sin_ref[rows, :]

        def rot(t):
            x1, x2 = t[:, :half], t[:, half:]
            return jnp.concatenate([x1 * cos - x2 * sin, x1 * sin + x2 * cos], axis=-1)

        q = rot(q_ref[0, rows, :])
        k = rot(k_ref[0, rows, :]) * (dk ** -0.5)
        vb = v_ref[0, rows, :].astype(BF16)
        qb = q.astype(BF16)
        s = lax.dot_general(qb, k.astype(BF16), _NT, preferred_element_type=F32) * decay_ref[0]
        inner = jnp.dot(s.astype(BF16), vb, preferred_element_type=F32)
        cross = jnp.dot(qb, state.astype(BF16), preferred_element_type=F32) * qdec_ref[0]
        kd_t = (k * kdec_ref[0]).T.astype(BF16)
        state = state * cdec_ref[0] + jnp.dot(kd_t, vb, preferred_element_type=F32)
        o = inner + cross
        mu = jnp.mean(o, axis=-1, keepdims=True)
        oc = o - mu
        var = jnp.mean(oc * oc, axis=-1, keepdims=True)
        on = oc * lax.rsqrt(var + LN_EPS)
        g = g_ref[0, rows, :]
        o_ref[0, rows, :] = (g * jax.nn.sigmoid(g) * on).astype(o_ref.dtype)
    state_ref[...] = state


def _retention(proj, tables, cps=4):
    B, S, n_in = proj.shape
    H, C = RET_HEADS, RET_CHUNK
    dk = n_in // (6 * H)
    dv = 2 * dk
    cos, sin, decay, qdec, kdec, cdec = tables
    T = cps * C
    return pl.pallas_call(
        functools.partial(_ret_kernel, dk=dk, cps=cps),
        grid=(B, H, S // T),
        in_specs=[
            pl.BlockSpec((1, T, dk), lambda b, h, c: (b, c, h)),
            pl.BlockSpec((1, T, dk), lambda b, h, c: (b, c, H + h)),
            pl.BlockSpec((1, T, dv), lambda b, h, c: (b, c, H + h)),
            pl.BlockSpec((1, T, dv), lambda b, h, c: (b, c, 2 * H + h)),
            pl.BlockSpec((T, dk // 2), lambda b, h, c: (c, 0)),
            pl.BlockSpec((T, dk // 2), lambda b, h, c: (c, 0)),
            pl.BlockSpec((1, C, C), lambda b, h, c: (h, 0, 0)),
            pl.BlockSpec((1, C, dv), lambda b, h, c: (h, 0, 0)),
            pl.BlockSpec((1, C, dk), lambda b, h, c: (h, 0, 0)),
            pl.BlockSpec((1, 1, dv), lambda b, h, c: (h, 0, 0)),
        ],
        out_specs=pl.BlockSpec((1, T, dv), lambda b, h, c: (b, c, h)),
        out_shape=jax.ShapeDtypeStruct((B, S, H * dv), BF16),
        scratch_shapes=[pltpu.VMEM((dk, dv), F32)],
        compiler_params=_params("parallel", "parallel", "arbitrary"),
        name="retention",
    )(proj, proj, proj, proj, cos, sin, decay, qdec, kdec, cdec)


def _retention_tables(S, dk):
    H, C = RET_HEADS, RET_CHUNK
    dv = 2 * dk
    pos = jnp.arange(S, dtype=F32)
    theta = 1.0 / (10000.0 ** jnp.linspace(0.0, 1.0, dk // 2, dtype=F32))
    ang = pos[:, None] * theta[None, :]
    log_g = jnp.log1p(-jnp.exp2(-5.0 - jnp.arange(H, dtype=F32)))
    idx = jnp.arange(C, dtype=F32)
    diff = idx[:, None] - idx[None, :]
    decay = jnp.where(diff >= 0, jnp.exp(jnp.maximum(diff, 0.0)[None] * log_g[:, None, None]), 0.0)
    q_dec = jnp.exp((idx + 1.0)[None] * log_g[:, None])
    k_dec = jnp.exp((C - 1.0 - idx)[None] * log_g[:, None])
    c_dec = jnp.exp(C * log_g)
    return (jnp.cos(ang), jnp.sin(ang), decay,
            jnp.broadcast_to(q_dec[:, :, None], (H, C, dv)),
            jnp.broadcast_to(k_dec[:, :, None], (H, C, dk)),
            jnp.broadcast_to(c_dec[:, None, None], (H, 1, dv)))


def _oproj_ln_kernel(a_ref, w_ref, x_ref, g_ref, lg_ref, lb_ref, o_ref):
    y = jnp.dot(a_ref[0], w_ref[...], preferred_element_type=F32)
    z = ALPHA * x_ref[0] + g_ref[0] * y
    o_ref[0] = _layer_norm(z, lg_ref[...], lb_ref[...])


def _oproj_ln(a, w, x, gate, lg, lb, tm=512):
    B, S, D = x.shape
    K = a.shape[-1]
    vec = pl.BlockSpec((1, 1, D), lambda b, s: (b, 0, 0))
    par = pl.BlockSpec((1, D), lambda b, s: (0, 0))
    return pl.pallas_call(
        _oproj_ln_kernel,
        grid=(B, S // tm),
        in_specs=[pl.BlockSpec((1, tm, K), lambda b, s: (b, s, 0)),
                  pl.BlockSpec((K, D), lambda b, s: (0, 0)),
                  pl.BlockSpec((1, tm, D), lambda b, s: (b, s, 0)),
                  vec, par, par],
        out_specs=pl.BlockSpec((1, tm, D), lambda b, s: (b, s, 0)),
        out_shape=jax.ShapeDtypeStruct((B, S, D), F32),
        compiler_params=_params("parallel", "parallel"),
        name="oproj_ln",
    )(a, w, x, gate, lg.reshape(1, D), lb.reshape(1, D))


def _top_values(s, k):
    vals = []
    for _ in range(k):
        m = jnp.max(s, axis=0, keepdims=True)
        vals.append(m)
        s = jnp.where(s == m, -jnp.inf, s)
    return vals


def _top_values_ranked(s, k):
    vals = []
    rank = jnp.full(s.shape, float(k), F32)
    for i in range(k):
        m = jnp.max(s, axis=0, keepdims=True)
        vals.append(m)
        hit = s == m
        rank = jnp.where(hit, float(i), rank)
        s = jnp.where(hit, -jnp.inf, s)
    return vals, rank


def _twice_bf16(x):
    b = pltpu.bitcast(x, jnp.uint32)
    return b | (b >> 16)


_PEER_PAIRS = [(p, q) for p in range(PEER_TOPK) for q in range(PEER_TOPK) if (p + 1) * (q + 1) <= PEER_TOPK + 1]
_PEER_CAND_ROWS = -(-len(_PEER_PAIRS) // 8) * 8


def _peer_kernel(x_ref, sc_ref, sh_ref, g_ref, wq_ref, keys_ref, u_ref, vt_ref, lg_ref, lb_ref, o_ref,
                 h_ref, w1w_ref, lw_ref, w2b_ref, r2b_ref, cand_ref, a_ref, ws_ref, p_ref, acc_ref, *, te, sub, kchunk, wb):
    e = pl.program_id(2)
    tm = h_ref.shape[0]

    @pl.when(e == 0)
    def _route():
        h = (x_ref[0] * (1.0 + sc_ref[0]) + sh_ref[0]).astype(BF16)
        h_ref[...] = h
        q_t = lax.dot_general(wq_ref[...], h, _NT, preferred_element_type=F32).astype(BF16)
        dq = keys_ref.shape[2]
        cand_ref[...] = jnp.full(cand_ref.shape, -jnp.inf, F32)
        for hd in range(PEER_HEADS):
            s1_all = jnp.dot(keys_ref[2 * hd], q_t[(2 * hd) * dq:(2 * hd + 1) * dq], preferred_element_type=F32)
            s2_all = jnp.dot(keys_ref[2 * hd + 1], q_t[(2 * hd + 1) * dq:(2 * hd + 2) * dq], preferred_element_type=F32)
            for lt in range(tm // LANES):
                ln = slice(lt * LANES, (lt + 1) * LANES)
                s1, s2 = s1_all[:, ln], s2_all[:, ln]
                a1 = _top_values(s1, PEER_TOPK)
                a2, rank2 = _top_values_ranked(s2, PEER_TOPK)
                for i, (p, q) in enumerate(_PEER_PAIRS):
                    cand_ref[i:i + 1, ln] = a1[p] + a2[q]
                c = _top_values(cand_ref[:, ln], PEER_TOPK + 1)
                z = jnp.ones_like(c[0])
                for kk in range(1, PEER_TOPK):
                    z = z + jnp.exp(c[kk] - c[0])
                inv_z = 1.0 / z
                tau = 0.5 * (c[PEER_TOPK - 1] + c[PEER_TOPK])
                n_ok = jnp.zeros_like(s1)
                for q in range(PEER_TOPK):
                    n_ok = jnp.where(s1 >= tau - a2[q], float(q + 1), n_ok)
                w1 = (jnp.exp(s1 - a1[0]) * inv_z).astype(BF16).astype(F32)
                by_step = lambda t: t.reshape(w1w_ref.shape[1], w1w_ref.shape[2], LANES)
                w1w_ref[hd, :, :, ln] = by_step(_twice_bf16(w1))
                lw_ref[hd, :, :, ln] = by_step(_twice_bf16(n_ok))
                w2b_ref[hd, :, ln] = jnp.exp(s2 - a2[0]).astype(BF16)
                r2b_ref[hd, :, ln] = rank2.astype(BF16)
        acc_ref[...] = jnp.zeros_like(acc_ref)

    per_sub = sub // PEER_NKEYS
    n_sub = te // sub

    def activations(sb):
        a_ref[sb] = lax.dot_general(u_ref[sb * sub:(sb + 1) * sub, :], h_ref[...], _NT,
                                    preferred_element_type=F32)

    activations(0)
    for sb in range(n_sub):
        for ii in range(per_sub):
            k = sb * per_sub + ii
            for lt in range(tm // wb):
                ln = slice(lt * wb, (lt + 1) * wb)
                wsum = None
                for hd in range(PEER_HEADS):
                    row_bf16 = lambda ref: pltpu.bitcast(
                        jnp.broadcast_to(ref[hd, e, k:k + 1, ln], (PEER_NKEYS // 2, wb)), BF16)
                    keep = r2b_ref[hd, :, ln] < row_bf16(lw_ref)
                    term = jnp.where(keep, w2b_ref[hd, :, ln], jnp.zeros((), BF16)) * row_bf16(w1w_ref)
                    wsum = term if wsum is None else wsum + term
                ws_ref[ii * PEER_NKEYS:(ii + 1) * PEER_NKEYS, ln] = wsum
        if sb + 1 < n_sub:
            activations(sb + 1)
        for ii in range(per_sub):
            rows = slice(ii * PEER_NKEYS, (ii + 1) * PEER_NKEYS)
            p_ref[pl.ds(sb * sub + ii * PEER_NKEYS, PEER_NKEYS), :] = (
                _gelu_tanh(a_ref[sb, rows, :]).astype(BF16) * ws_ref[rows, :])
        done = (sb + 1) * sub
        if done % kchunk == 0:
            cols = slice(done - kchunk, done)
            acc_ref[...] += jnp.dot(vt_ref[:, cols], p_ref[cols, :], preferred_element_type=F32)

    @pl.when(e == pl.num_programs(2) - 1)
    def _finish():
        y = acc_ref[...].T
        z = ALPHA * x_ref[0] + g_ref[0] * y
        o_ref[0] = _layer_norm(z, lg_ref[...], lb_ref[...])


def _peer(x, sc, sh, gate, wq_t, keys, u, v_t, lg, lb, tm=512, te=2048, sub=256, kchunk=1024, wb=256):
    B, S, D = x.shape
    E = u.shape[0]
    nq = wq_t.shape[0]
    vec = pl.BlockSpec((1, 1, D), lambda b, s, e: (b, 0, 0))
    par = pl.BlockSpec((1, D), lambda b, s, e: (0, 0))
    return pl.pallas_call(
        functools.partial(_peer_kernel, te=te, sub=sub, kchunk=kchunk, wb=wb),
        grid=(B, S // tm, E // te),
        in_specs=[pl.BlockSpec((1, tm, D), lambda b, s, e: (b, s, 0)), vec, vec, vec,
                  pl.BlockSpec((nq, D), lambda b, s, e: (0, 0), pipeline_mode=pl.Buffered(1)),
                  pl.BlockSpec(keys.shape, lambda b, s, e: (0, 0, 0), pipeline_mode=pl.Buffered(1)),
                  pl.BlockSpec((te, D), lambda b, s, e: (e, 0)),
                  pl.BlockSpec((D, te), lambda b, s, e: (0, e)),
                  par, par],
        out_specs=pl.BlockSpec((1, tm, D), lambda b, s, e: (b, s, 0)),
        out_shape=jax.ShapeDtypeStruct((B, S, D), F32),
        scratch_shapes=[pltpu.VMEM((tm, D), BF16),
                        pltpu.VMEM((PEER_HEADS, E // te, te // PEER_NKEYS, tm), jnp.uint32),
                        pltpu.VMEM((PEER_HEADS, E // te, te // PEER_NKEYS, tm), jnp.uint32),
                        pltpu.VMEM((PEER_HEADS, PEER_NKEYS, tm), BF16),
                        pltpu.VMEM((PEER_HEADS, PEER_NKEYS, tm), BF16),
                        pltpu.VMEM((_PEER_CAND_ROWS, tm), F32),
                        pltpu.VMEM((te // sub, sub, tm), F32),
                        pltpu.VMEM((sub, tm), BF16),
                        pltpu.VMEM((te, tm), BF16),
                        pltpu.VMEM((D, tm), F32)],
        compiler_params=_params("parallel", "parallel", "arbitrary"),
        name="peer",
    )(x, sc, sh, gate, wq_t, keys, u, v_t, lg.reshape(1, D), lb.reshape(1, D))


def _compress_kernel(pc_ref, pe_ref, w1_ref, b1_ref, w2_ref, w2t_ref, o_ref, ot_ref):
    pc = pc_ref[0, 0, 0]
    half = pc.shape[1]
    lo = (pc + pe_ref[0, 0:1, :]).astype(BF16)
    hi = (pc + pe_ref[0, 1:2, :]).astype(BF16)
    a = jnp.dot(lo, w1_ref[0, :half, :], preferred_element_type=F32)
    b = jnp.dot(hi, w1_ref[0, half:, :], preferred_element_type=F32)
    b_next = pltpu.roll(b, pc.shape[0] - 1, 0)
    hid = jax.nn.gelu(a + b_next + b1_ref[0]).astype(BF16)
    o_ref[0, 0, 0] = jnp.dot(hid, w2_ref[0], preferred_element_type=F32).astype(o_ref.dtype)
    ot_ref[0, 0, 0] = lax.dot_general(w2t_ref[0], hid, _NT, preferred_element_type=F32).astype(ot_ref.dtype)


def _compress(pieces, pe2, w1, b1, w2, w2t):
    _, B, G, NP, F = pieces.shape
    Hd = w1.shape[2]
    hd = w2.shape[2]
    return pl.pallas_call(
        _compress_kernel,
        grid=(2, B, G),
        in_specs=[pl.BlockSpec((1, 1, 1, NP, F), lambda c, b, g: (c, b, g, 0, 0)),
                  pl.BlockSpec((1, 2, F), lambda c, b, g: (c, 0, 0)),
                  pl.BlockSpec((1, 2 * F, Hd), lambda c, b, g: (c, 0, 0)),
                  pl.BlockSpec((1, 1, Hd), lambda c, b, g: (c, 0, 0)),
                  pl.BlockSpec((1, Hd, hd), lambda c, b, g: (c, 0, 0)),
                  pl.BlockSpec((1, hd, Hd), lambda c, b, g: (c, 0, 0))],
        out_specs=[pl.BlockSpec((1, 1, 1, NP, hd), lambda c, b, g: (c, b, g, 0, 0)),
                   pl.BlockSpec((1, 1, 1, hd, NP), lambda c, b, g: (c, b, g, 0, 0))],
        out_shape=[jax.ShapeDtypeStruct((2, B, G, NP, hd), BF16),
                   jax.ShapeDtypeStruct((2, B, G, hd, NP), BF16)],
        compiler_params=_params("parallel", "parallel", "parallel"),
        name="compress",
    )(pieces, pe2, w1, b1, w2, w2t)


def _softmax_t(s, mask):
    sm = jnp.where(mask, s, MASKED)
    m = jnp.maximum(jnp.max(sm, axis=0, keepdims=True), NEG_INF)
    e = jnp.exp2(sm - m)
    return e, 1.0 / jnp.maximum(jnp.sum(e, axis=0, keepdims=True), 1e-30)


def _nsa_kernel(qt_ref, gl_ref, kc_ref, vct_ref, ks_ref, vst_ref, kw_ref, vwt_ref, c2s_ref, o_ref,
                sel_ref, sc_ref, sw_ref, sa_ref, sb_ref, pa_ref, pb_ref):
    R, hd, QB = NSA_REP, NSA_HEAD_DIM, Q_BLOCK
    g = pl.program_id(1)
    start = pl.program_id(2) * QB
    q4 = qt_ref[0] * LOG2E
    q_t = jnp.concatenate([q4[r * hd:(r + 1) * hd] for r in range(R)], axis=1).astype(BF16)
    t_q = start + lax.broadcasted_iota(jnp.int32, (1, QB), 1)
    lanes = lambda r: slice(r * QB, (r + 1) * QB)
    dot = lambda a, b: jnp.dot(a, b, preferred_element_type=F32)
    n_tiles = ks_ref.shape[2] // SLC_TILE

    def k_slc_tile(kt):
        return ks_ref[0, 0, pl.ds(pl.multiple_of(kt * SLC_TILE, SLC_TILE), SLC_TILE), :]

    base = pl.multiple_of(jnp.maximum(start - WINDOW, 0), QB)
    sc_ref[...] = dot(kc_ref[0, 0], q_t)
    sw_ref[...] = dot(kw_ref[0, 0, pl.ds(base, WIN_KEYS), :], q_t)
    sa_ref[...] = dot(k_slc_tile(0), q_t)

    n_cp = kc_ref.shape[2]
    n_idx = lax.broadcasted_iota(jnp.int32, (n_cp, 1), 0)
    cmask = (n_idx * CMP_STRIDE + (CMP_LEN - 1)) <= t_q
    v_cmp_t = vct_ref[0, 0]
    o_cmp, psum = [], None
    for r in range(R):
        e, inv = _softmax_t(sc_ref[:, lanes(r)], cmask)
        p = e * inv
        o_cmp.append(dot(v_cmp_t, p.astype(BF16)))
        psum = p if psum is None else psum + p

    kp = base + lax.broadcasted_iota(jnp.int32, (WIN_KEYS, 1), 0)
    wmask = (kp <= t_q) & (kp > t_q - WINDOW)
    o_win = []
    for r in range(R):
        e, inv = _softmax_t(sw_ref[:, lanes(r)], wmask)
        e = e.astype(BF16)
        acc = None
        for a in range(WIN_KEYS // QB):
            d = dot(vwt_ref[0, 0, base // QB + a], e[a * QB:(a + 1) * QB])
            acc = d if acc is None else acc + d
        o_win.append(acc * inv)

    c2s = c2s_ref[...]
    imp = None
    for part in _split3(psum):
        d = dot(c2s, part)
        imp = d if imp is None else imp + d
    n_slc = c2s.shape[0]
    j_idx = lax.broadcasted_iota(jnp.int32, (n_slc, 1), 0)
    cur = jnp.right_shift(t_q, SLC_BLOCK.bit_length() - 1)
    forced = (j_idx == 0) | (j_idx == cur) | (j_idx == cur - 1)
    avail = (j_idx * SLC_BLOCK) <= t_q
    score = jnp.where(avail, imp + jnp.where(forced, FORCE_BONUS, 0.0), -1.0)
    work = score
    sel = jnp.zeros(score.shape, F32)
    for _ in range(min(SLC_TOPK, n_slc)):
        mx = jnp.max(work, axis=0, keepdims=True)
        first = jnp.min(jnp.where(work == mx, j_idx, n_slc), axis=0, keepdims=True)
        hit = j_idx == first
        sel = jnp.where(hit, 1.0, sel)
        work = jnp.where(hit, -jnp.inf, work)
    sel_ref[...] = jnp.where(score >= 0.0, sel, 0.0)

    per_tile = SLC_TILE // SLC_BLOCK
    row_tok = lax.broadcasted_iota(jnp.int32, (SLC_TILE, 1), 0)
    last = start // SLC_TILE

    def tile_mask(kt, causal):
        blk = pl.multiple_of(jnp.minimum(kt, n_tiles - 1) * per_tile, per_tile)
        sel_rows = sel_ref[pl.ds(blk, per_tile), :]
        mask = jnp.concatenate(
            [jnp.broadcast_to(sel_rows[a:a + 1, :], (SLC_BLOCK, QB)) for a in range(per_tile)], axis=0) > 0.5
        if causal:
            mask = mask & ((kt * SLC_TILE + row_tok) <= t_q)
        return mask

    def softmax_update(s_buf, p_buf, mask, m, l):
        m_out, l_out, alphas = [], [], []
        for r in range(R):
            sm = jnp.where(mask, s_buf[:, lanes(r)], MASKED)
            m_new = jnp.maximum(m[r], jnp.max(sm, axis=0, keepdims=True))
            p = jnp.exp2(sm - m_new)
            alphas.append(jnp.exp2(m[r] - m_new))
            p_buf[:, lanes(r)] = p.astype(BF16)
            m_out.append(m_new)
            l_out.append(l[r] * alphas[r] + jnp.sum(p, axis=0, keepdims=True))
        return tuple(m_out), tuple(l_out), jnp.concatenate(alphas, axis=1)

    v_tile_t = lambda kt: vst_ref[0, 0, jnp.clip(kt, 0, n_tiles - 1)]

    def pair_step(t0, carry, causal):
        m, l, acc, alpha_prev = carry
        sb_ref[...] = dot(k_slc_tile(jnp.minimum(t0 + 1, n_tiles - 1)), q_t)
        m, l, alpha0 = softmax_update(sa_ref, pa_ref, tile_mask(t0, causal), m, l)
        acc = acc * alpha_prev + dot(v_tile_t(t0 - 1), pb_ref[...])
        sa_ref[...] = dot(k_slc_tile(jnp.minimum(t0 + 2, n_tiles - 1)), q_t)
        m, l, alpha1 = softmax_update(sb_ref, pb_ref, tile_mask(t0 + 1, causal), m, l)
        acc = acc * alpha0 + dot(v_tile_t(t0), pa_ref[...])
        return m, l, acc, alpha1

    pb_ref[...] = jnp.zeros_like(pb_ref)
    init = (tuple(jnp.full((1, QB), NEG_INF, F32) for _ in range(R)),
            tuple(jnp.zeros((1, QB), F32) for _ in range(R)),
            jnp.zeros((hd, R * QB), F32), jnp.ones((1, R * QB), F32))
    n_pairs = last // 2
    carry = lax.fori_loop(0, n_pairs, lambda i, c: pair_step(2 * i, c, False), init)
    _, l, acc, alpha = pair_step(2 * n_pairs, carry, True)
    acc = acc * alpha + dot(v_tile_t(2 * n_pairs + 1), pb_ref[...])
    o_slc = [acc[:, lanes(r)] * (1.0 / jnp.maximum(l[r], 1e-30)) for r in range(R)]

    outs = []
    for r in range(R):
        row = (g * R + r) * N_BRANCH
        gate = [jax.nn.sigmoid(gl_ref[0, pl.ds(row + br, 1), :]) for br in range(N_BRANCH)]
        outs.append(gate[0] * o_cmp[r] + gate[1] * o_slc[r] + gate[2] * o_win[r])
    o_ref[0] = jnp.concatenate(outs, axis=0).T.astype(o_ref.dtype)


def _nsa_attention(proj_t, shared, c2s_t):
    k_cmp, v_cmp_t, k_slc, v_slc_t, k_win, v_win_t = shared
    B, _, S = proj_t.shape
    G, R, hd, QB = NSA_GROUPS, NSA_REP, NSA_HEAD_DIM, Q_BLOCK
    n_cp = k_cmp.shape[2]
    n_slc = S // SLC_BLOCK
    gate_blk = (NSA_HEADS * hd) // LANES
    whole = lambda shape: pl.BlockSpec((1, 1) + shape, lambda b, g, q: (b, g) + (0,) * len(shape))
    return pl.pallas_call(
        _nsa_kernel,
        grid=(B, G, S // QB),
        in_specs=[pl.BlockSpec((1, R * hd, QB), lambda b, g, q: (b, g, q)),
                  pl.BlockSpec((1, LANES, QB), lambda b, g, q: (b, gate_blk, q)),
                  whole((n_cp, hd)), whole((hd, n_cp)),
                  whole((S, hd)), whole((S // SLC_TILE, hd, SLC_TILE)),
                  whole((S, hd)), whole((S // QB, hd, QB)),
                  pl.BlockSpec((n_slc, n_cp), lambda b, g, q: (0, 0))],
        out_specs=pl.BlockSpec((1, QB, R * hd), lambda b, g, q: (b, q, g)),
        out_shape=jax.ShapeDtypeStruct((B, S, NSA_HEADS * hd), BF16),
        scratch_shapes=[pltpu.VMEM((n_slc, QB), F32),
                        pltpu.VMEM((n_cp, R * QB), F32),
                        pltpu.VMEM((WIN_KEYS, R * QB), F32),
                        pltpu.VMEM((SLC_TILE, R * QB), F32),
                        pltpu.VMEM((SLC_TILE, R * QB), F32),
                        pltpu.VMEM((SLC_TILE, R * QB), BF16),
                        pltpu.VMEM((SLC_TILE, R * QB), BF16)],
        compiler_params=_params("parallel", "parallel", "arbitrary"),
        name="nsa_attention",
    )(proj_t, proj_t, k_cmp, v_cmp_t, k_slc, v_slc_t, k_win, v_win_t, c2s_t)


def _cmp_to_slc_t(n_cp, n_slc):
    i = np.arange(n_cp)[None, :] * CMP_STRIDE
    j = np.arange(n_slc)[:, None] * SLC_BLOCK
    ov = np.clip(np.minimum(i + CMP_LEN, j + SLC_BLOCK) - np.maximum(i, j), 0, None) / CMP_LEN
    ov[:, n_cp - 1] = 0.0
    return jnp.asarray(ov, dtype=BF16)


def _nsa_shared_kv(x, kv_sc, kv_sh, w_kv, cmp_pe, cmp_w1, cmp_b1, cmp_w2):
    B, S, _ = x.shape
    G, hd = NSA_GROUPS, NSA_HEAD_DIM
    kv = _modmm(x, kv_sc, kv_sh, w_kv.astype(BF16), transposed=False, tn=512)
    kv = kv.reshape(B, S, N_BRANCH, 2, G, hd)
    NP = S // CMP_STRIDE
    pieces = kv[:, :, 0].reshape(B, NP, CMP_STRIDE, 2, G, hd).transpose(3, 0, 4, 1, 2, 5)
    pieces = pieces.reshape(2, B, G, NP, CMP_STRIDE * hd)
    comp, comp_t = _compress(pieces, cmp_pe.reshape(2, 2, CMP_STRIDE * hd), cmp_w1.astype(BF16),
                             cmp_b1[:, None, :], cmp_w2.astype(BF16), cmp_w2.transpose(0, 2, 1).astype(BF16))
    to_k = lambda t: t.transpose(0, 2, 1, 3).astype(BF16)

    def to_vt(t, tile):
        return t.reshape(B, S // tile, tile, G, hd).transpose(0, 3, 1, 4, 2).astype(BF16)

    return (comp[0], comp_t[1], to_k(kv[:, :, 1, 0]), to_vt(kv[:, :, 1, 1], SLC_TILE),
            to_k(kv[:, :, 2, 0]), to_vt(kv[:, :, 2, 1], Q_BLOCK))


def kernel(x, c, ada_w, ada_b, ln_g, ln_b, ret_w_in, ret_w_o, kv_ada_w, kv_ada_b, nsa_w_kv, cmp_pe, cmp_w1, cmp_b1, cmp_w2, nsa_w_in, nsa_w_o, peer_w_q, peer_keys, peer_u, peer_v):
    B, S, D = x.shape
    mods = _cmod(c, ada_w, ada_b)
    kv_mod = _cmod(c, kv_ada_w[None], kv_ada_b[None])[0]
    ret_tables = _retention_tables(S, D // RET_HEADS)
    hd = NSA_HEAD_DIM
    n_gate = NSA_HEADS * N_BRANCH
    c2s_t = _cmp_to_slc_t(S // CMP_STRIDE, S // SLC_BLOCK)
    shared = None
    for layer in range(DEPTH):
        sh1, sc1, g1, sh2, sc2, g2 = [m[:, None, :] for m in jnp.split(mods[layer], 6, axis=-1)]
        if layer < N_A_LAYERS:
            proj = _modmm(x, sc1, sh1, ret_w_in[layer].astype(BF16), transposed=False, tn=1024, tm=1024)
            a = _retention(proj, ret_tables)
            w_o = ret_w_o[layer]
        else:
            lb = layer - N_A_LAYERS
            w_in = nsa_w_in[lb]
            w_t = jnp.concatenate([w_in[:, :NSA_HEADS * hd].T * (hd ** -0.5), w_in[:, NSA_HEADS * hd:].T,
                                   jnp.zeros((LANES - n_gate, D), F32)], axis=0).astype(BF16)
            proj_t = _modmm(x, sc1, sh1, w_t, transposed=True, tn=384)
            a = _nsa_attention(proj_t, shared, c2s_t)
            w_o = nsa_w_o[lb]
        x = _oproj_ln(a, w_o.astype(BF16), x, g1, ln_g[layer, 0], ln_b[layer, 0])
        keys = peer_keys[layer].reshape(PEER_HEADS * 2, PEER_NKEYS, -1).astype(BF16)
        x = _peer(x, sc2, sh2, g2, peer_w_q[layer].T.astype(BF16), keys, peer_u[layer].astype(BF16),
                  peer_v[layer].T.astype(BF16), ln_g[layer, 1], ln_b[layer, 1])
        if layer == N_A_LAYERS - 1:
            kv_sh, kv_sc = [m[:, None, :] for m in jnp.split(kv_mod, 2, axis=-1)]
            shared = _nsa_shared_kv(x, kv_sc, kv_sh, nsa_w_kv, cmp_pe, cmp_w1, cmp_b1, cmp_w2)
    return x
```

```python
import functools

import numpy as np
import jax
import jax.numpy as jnp
from jax import lax
from jax.experimental import pallas as pl
from jax.experimental.pallas import tpu as pltpu

DEPTH = 4
N_A_LAYERS = DEPTH // 2
ALPHA = (2.0 * DEPTH) ** 0.25
LN_EPS = 1e-5
NEG_INF = -1e30
MASKED = 2.0 * NEG_INF
LOG2E = 1.4426950408889634

RET_HEADS = 4
RET_CHUNK = 128

NSA_HEADS = 16
NSA_GROUPS = 4
NSA_REP = NSA_HEADS // NSA_GROUPS
NSA_HEAD_DIM = 64
N_BRANCH = 3
CMP_STRIDE = 16
CMP_LEN = 32
SLC_BLOCK = 64
SLC_TOPK = 16
WINDOW = 512
Q_BLOCK = 128
FORCE_BONUS = 100.0
SLC_TILE = 512
WIN_KEYS = WINDOW + Q_BLOCK

PEER_HEADS = 8
PEER_NKEYS = 128
PEER_TOPK = 16

LANES = 128
VMEM_LIMIT = 56 * 1024 * 1024

F32 = jnp.float32
BF16 = jnp.bfloat16
_NT = (((1,), (1,)), ((), ()))


def _params(*sem):
    return pltpu.CompilerParams(dimension_semantics=sem, vmem_limit_bytes=VMEM_LIMIT)


def _split3(a):
    hi = a.astype(BF16)
    r1 = a - hi.astype(F32)
    mid = r1.astype(BF16)
    lo = (r1 - mid.astype(F32)).astype(BF16)
    return hi, mid, lo


def _layer_norm(z, g, b):
    mu = jnp.mean(z, axis=-1, keepdims=True)
    zc = z - mu
    var = jnp.mean(zc * zc, axis=-1, keepdims=True)
    return zc * lax.rsqrt(var + LN_EPS) * g + b


_GELU_K1 = -2.0 * 0.7978845608028654 * LOG2E
_GELU_K3 = _GELU_K1 * 0.044715


def _gelu_tanh(x):
    return x * (1.0 / (1.0 + jnp.exp2(x * (_GELU_K3 * (x * x) + _GELU_K1))))


def _cmod_kernel(c_ref, w_ref, b_ref, o_ref):
    c = c_ref[...]
    ca = c * jax.nn.sigmoid(c)
    w = w_ref[0]
    c_hi, c_mid, c_lo = _split3(ca)
    w_hi, w_mid, w_lo = _split3(w)
    dot = lambda a, b: jnp.dot(a, b, preferred_element_type=F32)
    acc = dot(c_hi, w_hi) + (dot(c_hi, w_mid) + dot(c_mid, w_hi)) + (dot(c_hi, w_lo) + dot(c_mid, w_mid) + dot(c_lo, w_hi))
    o_ref[0] = acc + b_ref[0]


def _cmod(c, w, b):
    L, D, N = w.shape
    n_b = c.shape[0]
    B = 16
    c = jnp.zeros((B, D), F32).at[:n_b].set(c)
    tn = 512
    out = pl.pallas_call(
        _cmod_kernel,
        grid=(L, N // tn),
        in_specs=[pl.BlockSpec((B, D), lambda l, n: (0, 0)),
                  pl.BlockSpec((1, D, tn), lambda l, n: (l, 0, n)),
                  pl.BlockSpec((1, 1, tn), lambda l, n: (l, 0, n))],
        out_specs=pl.BlockSpec((1, B, tn), lambda l, n: (l, 0, n)),
        out_shape=jax.ShapeDtypeStruct((L, B, N), F32),
        compiler_params=_params("parallel", "parallel"),
        name="cmod",
    )(c, w, b.reshape(L, 1, N))
    return out[:, :n_b]


def _modmm_kernel(x_ref, sc_ref, sh_ref, w_ref, o_ref, h_ref, *, transposed):
    @pl.when(pl.program_id(2) == 0)
    def _():
        h = x_ref[0] * (1.0 + sc_ref[0]) + sh_ref[0]
        h_ref[...] = h.astype(BF16)

    if transposed:
        o = lax.dot_general(w_ref[...], h_ref[...], _NT, preferred_element_type=F32)
    else:
        o = jnp.dot(h_ref[...], w_ref[...], preferred_element_type=F32)
    o_ref[0] = o.astype(o_ref.dtype)


def _modmm(x, sc, sh, w, *, transposed, tn, out_dtype=F32, tm=512):
    B, S, D = x.shape
    N = w.shape[0] if transposed else w.shape[1]
    if transposed:
        w_spec = pl.BlockSpec((tn, D), lambda b, s, n: (n, 0))
        o_spec = pl.BlockSpec((1, tn, tm), lambda b, s, n: (b, n, s))
        o_shape = (B, N, S)
    else:
        w_spec = pl.BlockSpec((D, tn), lambda b, s, n: (0, n))
        o_spec = pl.BlockSpec((1, tm, tn), lambda b, s, n: (b, s, n))
        o_shape = (B, S, N)
    vec = pl.BlockSpec((1, 1, D), lambda b, s, n: (b, 0, 0))
    return pl.pallas_call(
        functools.partial(_modmm_kernel, transposed=transposed),
        grid=(B, S // tm, N // tn),
        in_specs=[pl.BlockSpec((1, tm, D), lambda b, s, n: (b, s, 0)), vec, vec, w_spec],
        out_specs=o_spec,
        out_shape=jax.ShapeDtypeStruct(o_shape, out_dtype),
        scratch_shapes=[pltpu.VMEM((tm, D), BF16)],
        compiler_params=_params("parallel", "parallel", "arbitrary"),
        name="modmm_t" if transposed else "modmm",
    )(x, sc, sh, w)


def _ret_kernel(q_ref, k_ref, v_ref, g_ref, cos_ref, sin_ref, decay_ref, qdec_ref, kdec_ref, cdec_ref,
                o_ref, state_ref, *, dk, cps):
    @pl.when(pl.program_id(2) == 0)
    def _():
        state_ref[...] = jnp.zeros_like(state_ref)

    C = RET_CHUNK
    half = dk // 2
    state = state_ref[...]
    for j in range(cps):
        rows = slice(j * C, (j + 1) * C)
        cos, sin = cos_ref[rows, :], sin_ref[rows, :]

        def rot(t):
            x1, x2 = t[:, :half], t[:, half:]
            return jnp.concatenate([x1 * cos - x2 * sin, x1 * sin + x2 * cos], axis=-1)

        q = rot(q_ref[0, rows, :])
        k = rot(k_ref[0, rows, :]) * (dk ** -0.5)
        vb = v_ref[0, rows, :].astype(BF16)
        qb = q.astype(BF16)
        s = lax.dot_general(qb, k.astype(BF16), _NT, preferred_element_type=F32) * decay_ref[0]
        inner = jnp.dot(s.astype(BF16), vb, preferred_element_type=F32)
        cross = jnp.dot(qb, state.astype(BF16), preferred_element_type=F32) * qdec_ref[0]
        kd_t = (k * kdec_ref[0]).T.astype(BF16)
        state = state * cdec_ref[0] + jnp.dot(kd_t, vb, preferred_element_type=F32)
        o = inner + cross
        mu = jnp.mean(o, axis=-1, keepdims=True)
        oc = o - mu
        var = jnp.mean(oc * oc, axis=-1, keepdims=True)
        on = oc * lax.rsqrt(var + LN_EPS)
        g = g_ref[0, rows, :]
        o_ref[0, rows, :] = (g * jax.nn.sigmoid(g) * on).astype(o_ref.dtype)
    state_ref[...] = state


def _retention(proj, tables, cps=4):
    B, S, n_in = proj.shape
    H, C = RET_HEADS, RET_CHUNK
    dk = n_in // (6 * H)
    dv = 2 * dk
    cos, sin, decay, qdec, kdec, cdec = tables
    T = cps * C
    return pl.pallas_call(
        functools.partial(_ret_kernel, dk=dk, cps=cps),
        grid=(B, H, S // T),
        in_specs=[
            pl.BlockSpec((1, T, dk), lambda b, h, c: (b, c, h)),
            pl.BlockSpec((1, T, dk), lambda b, h, c: (b, c, H + h)),
            pl.BlockSpec((1, T, dv), lambda b, h, c: (b, c, H + h)),
            pl.BlockSpec((1, T, dv), lambda b, h, c: (b, c, 2 * H + h)),
            pl.BlockSpec((T, dk // 2), lambda b, h, c: (c, 0)),
            pl.BlockSpec((T, dk // 2), lambda b, h, c: (c, 0)),
            pl.BlockSpec((1, C, C), lambda b, h, c: (h, 0, 0)),
            pl.BlockSpec((1, C, dv), lambda b, h, c: (h, 0, 0)),
            pl.BlockSpec((1, C, dk), lambda b, h, c: (h, 0, 0)),
            pl.BlockSpec((1, 1, dv), lambda b, h, c: (h, 0, 0)),
        ],
        out_specs=pl.BlockSpec((1, T, dv), lambda b, h, c: (b, c, h)),
        out_shape=jax.ShapeDtypeStruct((B, S, H * dv), BF16),
        scratch_shapes=[pltpu.VMEM((dk, dv), F32)],
        compiler_params=_params("parallel", "parallel", "arbitrary"),
        name="retention",
    )(proj, proj, proj, proj, cos, sin, decay, qdec, kdec, cdec)


def _retention_tables(S, dk):
    H, C = RET_HEADS, RET_CHUNK
    dv = 2 * dk
    pos = jnp.arange(S, dtype=F32)
    theta = 1.0 / (10000.0 ** jnp.linspace(0.0, 1.0, dk // 2, dtype=F32))
    ang = pos[:, None] * theta[None, :]
    log_g = jnp.log1p(-jnp.exp2(-5.0 - jnp.arange(H, dtype=F32)))
    idx = jnp.arange(C, dtype=F32)
    diff = idx[:, None] - idx[None, :]
    decay = jnp.where(diff >= 0, jnp.exp(jnp.maximum(diff, 0.0)[None] * log_g[:, None, None]), 0.0)
    q_dec = jnp.exp((idx + 1.0)[None] * log_g[:, None])
    k_dec = jnp.exp((C - 1.0 - idx)[None] * log_g[:, None])
    c_dec = jnp.exp(C * log_g)
    return (jnp.cos(ang), jnp.sin(ang), decay,
            jnp.broadcast_to(q_dec[:, :, None], (H, C, dv)),
            jnp.broadcast_to(k_dec[:, :, None], (H, C, dk)),
            jnp.broadcast_to(c_dec[:, None, None], (H, 1, dv)))


def _oproj_ln_kernel(a_ref, w_ref, x_ref, g_ref, lg_ref, lb_ref, o_ref):
    y = jnp.dot(a_ref[0], w_ref[...], preferred_element_type=F32)
    z = ALPHA * x_ref[0] + g_ref[0] * y
    o_ref[0] = _layer_norm(z, lg_ref[...], lb_ref[...])


def _oproj_ln(a, w, x, gate, lg, lb, tm=512):
    B, S, D = x.shape
    K = a.shape[-1]
    vec = pl.BlockSpec((1, 1, D), lambda b, s: (b, 0, 0))
    par = pl.BlockSpec((1, D), lambda b, s: (0, 0))
    return pl.pallas_call(
        _oproj_ln_kernel,
        grid=(B, S // tm),
        in_specs=[pl.BlockSpec((1, tm, K), lambda b, s: (b, s, 0)),
                  pl.BlockSpec((K, D), lambda b, s: (0, 0)),
                  pl.BlockSpec((1, tm, D), lambda b, s: (b, s, 0)),
                  vec, par, par],
        out_specs=pl.BlockSpec((1, tm, D), lambda b, s: (b, s, 0)),
        out_shape=jax.ShapeDtypeStruct((B, S, D), F32),
        compiler_params=_params("parallel", "parallel"),
        name="oproj_ln",
    )(a, w, x, gate, lg.reshape(1, D), lb.reshape(1, D))


def _top_values(s, k):
    vals = []
    for _ in range(k):
        m = jnp.max(s, axis=0, keepdims=True)
        vals.append(m)
        s = jnp.where(s == m, -jnp.inf, s)
    return vals


def _top_values_ranked(s, k):
    vals = []
    rank = jnp.full(s.shape, float(k), F32)
    for i in range(k):
        m = jnp.max(s, axis=0, keepdims=True)
        vals.append(m)
        hit = s == m
        rank = jnp.where(hit, float(i), rank)
        s = jnp.where(hit, -jnp.inf, s)
    return vals, rank


def _twice_bf16(x):
    b = pltpu.bitcast(x, jnp.uint32)
    return b | (b >> 16)


_PEER_PAIRS = [(p, q) for p in range(PEER_TOPK) for q in range(PEER_TOPK) if (p + 1) * (q + 1) <= PEER_TOPK + 1]
_PEER_CAND_ROWS = -(-len(_PEER_PAIRS) // 8) * 8


def _peer_kernel(x_ref, sc_ref, sh_ref, g_ref, wq_ref, keys_ref, u_ref, vt_ref, lg_ref, lb_ref, o_ref,
                 h_ref, w1w_ref, lw_ref, w2b_ref, r2b_ref, cand_ref, a_ref, ws_ref, p_ref, acc_ref, *, te, sub, kchunk, wb):
    e = pl.program_id(2)
    tm = h_ref.shape[0]

    @pl.when(e == 0)
    def _route():
        h = (x_ref[0] * (1.0 + sc_ref[0]) + sh_ref[0]).astype(BF16)
        h_ref[...] = h
        q_t = lax.dot_general(wq_ref[...], h, _NT, preferred_element_type=F32).astype(BF16)
        dq = keys_ref.shape[2]
        cand_ref[...] = jnp.full(cand_ref.shape, -jnp.inf, F32)
        for hd in range(PEER_HEADS):
            s1_all = jnp.dot(keys_ref[2 * hd], q_t[(2 * hd) * dq:(2 * hd + 1) * dq], preferred_element_type=F32)
            s2_all = jnp.dot(keys_ref[2 * hd + 1], q_t[(2 * hd + 1) * dq:(2 * hd + 2) * dq], preferred_element_type=F32)
            for lt in range(tm // LANES):
                ln = slice(lt * LANES, (lt + 1) * LANES)
                s1, s2 = s1_all[:, ln], s2_all[:, ln]
                a1 = _top_values(s1, PEER_TOPK)
                a2, rank2 = _top_values_ranked(s2, PEER_TOPK)
                for i, (p, q) in enumerate(_PEER_PAIRS):
                    cand_ref[i:i + 1, ln] = a1[p] + a2[q]
                c = _top_values(cand_ref[:, ln], PEER_TOPK + 1)
                z = jnp.ones_like(c[0])
                for kk in range(1, PEER_TOPK):
                    z = z + jnp.exp(c[kk] - c[0])
                inv_z = 1.0 / z
                tau = 0.5 * (c[PEER_TOPK - 1] + c[PEER_TOPK])
                n_ok = jnp.zeros_like(s1)
                for q in range(PEER_TOPK):
                    n_ok = jnp.where(s1 >= tau - a2[q], float(q + 1), n_ok)
                w1 = (jnp.exp(s1 - a1[0]) * inv_z).astype(BF16).astype(F32)
                by_step = lambda t: t.reshape(w1w_ref.shape[1], w1w_ref.shape[2], LANES)
                w1w_ref[hd, :, :, ln] = by_step(_twice_bf16(w1))
                lw_ref[hd, :, :, ln] = by_step(_twice_bf16(n_ok))
                w2b_ref[hd, :, ln] = jnp.exp(s2 - a2[0]).astype(BF16)
                r2b_ref[hd, :, ln] = rank2.astype(BF16)
        acc_ref[...] = jnp.zeros_like(acc_ref)

    per_sub = sub // PEER_NKEYS
    n_sub = te // sub

    def activations(sb):
        a_ref[sb] = lax.dot_general(u_ref[sb * sub:(sb + 1) * sub, :], h_ref[...], _NT,
                                    preferred_element_type=F32)

    activations(0)
    for sb in range(n_sub):
        for ii in range(per_sub):
            k = sb * per_sub + ii
            for lt in range(tm // wb):
                ln = slice(lt * wb, (lt + 1) * wb)
                wsum = None
                for hd in range(PEER_HEADS):
                    row_bf16 = lambda ref: pltpu.bitcast(
                        jnp.broadcast_to(ref[hd, e, k:k + 1, ln], (PEER_NKEYS // 2, wb)), BF16)
                    keep = r2b_ref[hd, :, ln] < row_bf16(lw_ref)
                    term = jnp.where(keep, w2b_ref[hd, :, ln], jnp.zeros((), BF16)) * row_bf16(w1w_ref)
                    wsum = term if wsum is None else wsum + term
                ws_ref[ii * PEER_NKEYS:(ii + 1) * PEER_NKEYS, ln] = wsum
        if sb + 1 < n_sub:
            activations(sb + 1)
        for ii in range(per_sub):
            rows = slice(ii * PEER_NKEYS, (ii + 1) * PEER_NKEYS)
            p_ref[pl.ds(sb * sub + ii * PEER_NKEYS, PEER_NKEYS), :] = (
                _gelu_tanh(a_ref[sb, rows, :]).astype(BF16) * ws_ref[rows, :])
        done = (sb + 1) * sub
        if done % kchunk == 0:
            cols = slice(done - kchunk, done)
            acc_ref[...] += jnp.dot(vt_ref[:, cols], p_ref[cols, :], preferred_element_type=F32)

    @pl.when(e == pl.num_programs(2) - 1)
    def _finish():
        y = acc_ref[...].T
        z = ALPHA * x_ref[0] + g_ref[0] * y
        o_ref[0] = _layer_norm(z, lg_ref[...], lb_ref[...])


def _peer(x, sc, sh, gate, wq_t, keys, u, v_t, layer, lg, lb, tm=512, te=2048, sub=256, kchunk=1024, wb=256):
    B, S, D = x.shape
    E = u.shape[1]
    nq = wq_t.shape[0]
    vec = pl.BlockSpec((1, 1, D), lambda b, s, e: (b, 0, 0))
    par = pl.BlockSpec((1, D), lambda b, s, e: (0, 0))
    return pl.pallas_call(
        functools.partial(_peer_kernel, te=te, sub=sub, kchunk=kchunk, wb=wb),
        grid=(B, S // tm, E // te),
        in_specs=[pl.BlockSpec((1, tm, D), lambda b, s, e: (b, s, 0)), vec, vec, vec,
                  pl.BlockSpec((nq, D), lambda b, s, e: (0, 0), pipeline_mode=pl.Buffered(1)),
                  pl.BlockSpec(keys.shape, lambda b, s, e: (0, 0, 0), pipeline_mode=pl.Buffered(1)),
                  pl.BlockSpec((None, te, D), lambda b, s, e: (layer, e, 0)),
                  pl.BlockSpec((None, D, te), lambda b, s, e: (layer, 0, e)),
                  par, par],
        out_specs=pl.BlockSpec((1, tm, D), lambda b, s, e: (b, s, 0)),
        out_shape=jax.ShapeDtypeStruct((B, S, D), F32),
        scratch_shapes=[pltpu.VMEM((tm, D), BF16),
                        pltpu.VMEM((PEER_HEADS, E // te, te // PEER_NKEYS, tm), jnp.uint32),
                        pltpu.VMEM((PEER_HEADS, E // te, te // PEER_NKEYS, tm), jnp.uint32),
                        pltpu.VMEM((PEER_HEADS, PEER_NKEYS, tm), BF16),
                        pltpu.VMEM((PEER_HEADS, PEER_NKEYS, tm), BF16),
                        pltpu.VMEM((_PEER_CAND_ROWS, tm), F32),
                        pltpu.VMEM((te // sub, sub, tm), F32),
                        pltpu.VMEM((sub, tm), BF16),
                        pltpu.VMEM((te, tm), BF16),
                        pltpu.VMEM((D, tm), F32)],
        compiler_params=_params("parallel", "parallel", "arbitrary"),
        name="peer",
    )(x, sc, sh, gate, wq_t, keys, u, v_t, lg.reshape(1, D), lb.reshape(1, D))


def _compress_kernel(pc_ref, pe_ref, w1_ref, b1_ref, w2_ref, w2t_ref, o_ref, ot_ref):
    pc = pc_ref[0, 0, 0]
    half = pc.shape[1]
    lo = (pc + pe_ref[0, 0:1, :]).astype(BF16)
    hi = (pc + pe_ref[0, 1:2, :]).astype(BF16)
    a = jnp.dot(lo, w1_ref[0, :half, :], preferred_element_type=F32)
    b = jnp.dot(hi, w1_ref[0, half:, :], preferred_element_type=F32)
    b_next = pltpu.roll(b, pc.shape[0] - 1, 0)
    hid = jax.nn.gelu(a + b_next + b1_ref[0]).astype(BF16)
    o_ref[0, 0, 0] = jnp.dot(hid, w2_ref[0], preferred_element_type=F32).astype(o_ref.dtype)
    ot_ref[0, 0, 0] = lax.dot_general(w2t_ref[0], hid, _NT, preferred_element_type=F32).astype(ot_ref.dtype)


def _compress(pieces, pe2, w1, b1, w2, w2t):
    _, B, G, NP, F = pieces.shape
    Hd = w1.shape[2]
    hd = w2.shape[2]
    return pl.pallas_call(
        _compress_kernel,
        grid=(2, B, G),
        in_specs=[pl.BlockSpec((1, 1, 1, NP, F), lambda c, b, g: (c, b, g, 0, 0)),
                  pl.BlockSpec((1, 2, F), lambda c, b, g: (c, 0, 0)),
                  pl.BlockSpec((1, 2 * F, Hd), lambda c, b, g: (c, 0, 0)),
                  pl.BlockSpec((1, 1, Hd), lambda c, b, g: (c, 0, 0)),
                  pl.BlockSpec((1, Hd, hd), lambda c, b, g: (c, 0, 0)),
                  pl.BlockSpec((1, hd, Hd), lambda c, b, g: (c, 0, 0))],
        out_specs=[pl.BlockSpec((1, 1, 1, NP, hd), lambda c, b, g: (c, b, g, 0, 0)),
                   pl.BlockSpec((1, 1, 1, hd, NP), lambda c, b, g: (c, b, g, 0, 0))],
        out_shape=[jax.ShapeDtypeStruct((2, B, G, NP, hd), BF16),
                   jax.ShapeDtypeStruct((2, B, G, hd, NP), BF16)],
        compiler_params=_params("parallel", "parallel", "parallel"),
        name="compress",
    )(pieces, pe2, w1, b1, w2, w2t)


def _softmax_t(s, mask):
    sm = jnp.where(mask, s, MASKED)
    m = jnp.maximum(jnp.max(sm, axis=0, keepdims=True), NEG_INF)
    e = jnp.exp2(sm - m)
    return e, 1.0 / jnp.maximum(jnp.sum(e, axis=0, keepdims=True), 1e-30)


def _nsa_kernel(qt_ref, gl_ref, kc_ref, vct_ref, ks_ref, vst_ref, kw_ref, vwt_ref, c2s_ref, o_ref,
                sel_ref, sc_ref, sw_ref, sa_ref, sb_ref, pa_ref, pb_ref, acc_ref):
    R, hd, QB = NSA_REP, NSA_HEAD_DIM, Q_BLOCK
    g = pl.program_id(1)
    start = pl.program_id(2) * QB
    q4 = qt_ref[0] * LOG2E
    q_t = jnp.concatenate([q4[r * hd:(r + 1) * hd] for r in range(R)], axis=1).astype(BF16)
    t_q = start + lax.broadcasted_iota(jnp.int32, (1, QB), 1)
    lanes = lambda r: slice(r * QB, (r + 1) * QB)
    dot = lambda a, b: jnp.dot(a, b, preferred_element_type=F32)
    n_tiles = ks_ref.shape[2] // SLC_TILE

    def k_slc_tile(kt):
        return ks_ref[0, 0, pl.ds(pl.multiple_of(kt * SLC_TILE, SLC_TILE), SLC_TILE), :]

    base = pl.multiple_of(jnp.maximum(start - WINDOW, 0), QB)
    sc_ref[...] = dot(kc_ref[0, 0], q_t)
    sw_ref[...] = dot(kw_ref[0, 0, pl.ds(base, WIN_KEYS), :], q_t)
    sa_ref[...] = dot(k_slc_tile(0), q_t)

    n_cp = kc_ref.shape[2]
    n_idx = lax.broadcasted_iota(jnp.int32, (n_cp, 1), 0)
    cmask = (n_idx * CMP_STRIDE + (CMP_LEN - 1)) <= t_q
    v_cmp_t = vct_ref[0, 0]
    o_cmp, psum = [], None
    for r in range(R):
        e, inv = _softmax_t(sc_ref[:, lanes(r)], cmask)
        p = e * inv
        o_cmp.append(dot(v_cmp_t, p.astype(BF16)))
        psum = p if psum is None else psum + p

    kp = base + lax.broadcasted_iota(jnp.int32, (WIN_KEYS, 1), 0)
    wmask = (kp <= t_q) & (kp > t_q - WINDOW)

    def window_head(r):
        e, inv = _softmax_t(sw_ref[:, lanes(r)], wmask)
        return dot(vwt_ref[0, :, pl.ds(base, WIN_KEYS)], e.astype(BF16)) * inv

    c2s = c2s_ref[...]
    imp = None
    for part in _split3(psum):
        d = dot(c2s, part)
        imp = d if imp is None else imp + d
    n_slc = c2s.shape[0]
    j_idx = lax.broadcasted_iota(jnp.int32, (n_slc, 1), 0)
    cur = jnp.right_shift(t_q, SLC_BLOCK.bit_length() - 1)
    forced = (j_idx == 0) | (j_idx == cur) | (j_idx == cur - 1)
    avail = (j_idx * SLC_BLOCK) <= t_q
    score = jnp.where(avail, imp + jnp.where(forced, FORCE_BONUS, 0.0), -1.0)
    work = score
    sel = jnp.zeros(score.shape, F32)
    n_sel = min(SLC_TOPK, n_slc)
    o_win = []
    for it in range(n_sel):
        mx = jnp.max(work, axis=0, keepdims=True)
        first = jnp.min(jnp.where(work == mx, j_idx, n_slc), axis=0, keepdims=True)
        hit = j_idx == first
        sel = jnp.where(hit, 1.0, sel)
        work = jnp.where(hit, -jnp.inf, work)
        if (it + 1) % max(n_sel // R, 1) == 0 and len(o_win) < R:
            o_win.append(window_head(len(o_win)))
    o_win += [window_head(r) for r in range(len(o_win), R)]
    sel_ref[...] = jnp.where(score >= 0.0, sel, 0.0)

    per_tile = SLC_TILE // SLC_BLOCK
    row_tok = lax.broadcasted_iota(jnp.int32, (SLC_TILE, 1), 0)
    last = start // SLC_TILE

    def scores(kt):
        return dot(k_slc_tile(jnp.minimum(kt, n_tiles - 1)), q_t)

    def softmax_update(s_buf, p_buf, kt, causal, m, l):
        blk = pl.multiple_of(jnp.minimum(kt, n_tiles - 1) * per_tile, per_tile)
        sel_rows = sel_ref[pl.ds(blk, per_tile), :]
        mask = jnp.concatenate(
            [jnp.broadcast_to(sel_rows[a:a + 1, :], (SLC_BLOCK, QB)) for a in range(per_tile)], axis=0) > 0.5
        if causal:
            mask = mask & ((kt * SLC_TILE + row_tok) <= t_q)
        m_out, l_out, alphas = [], [], []
        for r in range(R):
            sm = jnp.where(mask, s_buf[:, lanes(r)], MASKED)
            m_new = jnp.maximum(m[r], jnp.max(sm, axis=0, keepdims=True))
            p = jnp.exp2(sm - m_new)
            alphas.append(jnp.exp2(m[r] - m_new))
            p_buf[:, lanes(r)] = p.astype(BF16)
            m_out.append(m_new)
            l_out.append(l[r] * alphas[r] + jnp.sum(p, axis=0, keepdims=True))
        return tuple(m_out), tuple(l_out), jnp.concatenate(alphas, axis=1)

    def v_tile_t(kt):
        k0 = pl.multiple_of(jnp.clip(kt, 0, n_tiles - 1) * SLC_TILE, SLC_TILE)
        return vst_ref[0, :, pl.ds(k0, SLC_TILE)]

    def pair_step(t0, carry, causal):
        m, l, alpha_prev = carry
        m, l, alpha0 = softmax_update(sa_ref, pa_ref, t0, causal, m, l)
        sb_ref[...] = scores(t0 + 1)
        acc_ref[...] = acc_ref[...] * alpha_prev + dot(v_tile_t(t0 - 1), pb_ref[...])
        m, l, alpha1 = softmax_update(sb_ref, pb_ref, t0 + 1, causal, m, l)
        sa_ref[...] = scores(t0 + 2)
        acc_ref[...] = acc_ref[...] * alpha0 + dot(v_tile_t(t0), pa_ref[...])
        return m, l, alpha1

    pb_ref[...] = jnp.zeros_like(pb_ref)
    acc_ref[...] = jnp.zeros_like(acc_ref)
    init = (tuple(jnp.full((1, QB), NEG_INF, F32) for _ in range(R)),
            tuple(jnp.zeros((1, QB), F32) for _ in range(R)),
            jnp.ones((1, R * QB), F32))
    n_pairs = last // 2
    carry = lax.fori_loop(0, n_pairs, lambda i, c: pair_step(2 * i, c, False), init)
    _, l, alpha = pair_step(2 * n_pairs, carry, True)
    acc = acc_ref[...] * alpha + dot(v_tile_t(2 * n_pairs + 1), pb_ref[...])
    o_slc = [acc[:, lanes(r)] * (1.0 / jnp.maximum(l[r], 1e-30)) for r in range(R)]

    outs = []
    for r in range(R):
        row = (g * R + r) * N_BRANCH
        gate = [jax.nn.sigmoid(gl_ref[0, pl.ds(row + br, 1), :]) for br in range(N_BRANCH)]
        outs.append(gate[0] * o_cmp[r] + gate[1] * o_slc[r] + gate[2] * o_win[r])
    o_ref[0] = jnp.concatenate(outs, axis=0).T.astype(o_ref.dtype)


def _nsa_attention(proj_t, shared, c2s_t):
    k_cmp, v_cmp_t, k_slc, k_win, v_t = shared
    B, _, S = proj_t.shape
    G, R, hd, QB = NSA_GROUPS, NSA_REP, NSA_HEAD_DIM, Q_BLOCK
    n_cp = k_cmp.shape[2]
    n_slc = S // SLC_BLOCK
    gate_blk = (NSA_HEADS * hd) // LANES
    whole = lambda shape: pl.BlockSpec((1, 1) + shape, lambda b, g, q: (b, g) + (0,) * len(shape))
    v_rows = lambda branch: pl.BlockSpec((1, hd, S), lambda b, g, q: (b, branch * G + g, 0))
    return pl.pallas_call(
        _nsa_kernel,
        grid=(B, G, S // QB),
        in_specs=[pl.BlockSpec((1, R * hd, QB), lambda b, g, q: (b, g, q)),
                  pl.BlockSpec((1, LANES, QB), lambda b, g, q: (b, gate_blk, q)),
                  whole((n_cp, hd)), whole((hd, n_cp)),
                  whole((S, hd)), v_rows(0),
                  whole((S, hd)), v_rows(1),
                  pl.BlockSpec((n_slc, n_cp), lambda b, g, q: (0, 0))],
        out_specs=pl.BlockSpec((1, QB, R * hd), lambda b, g, q: (b, q, g)),
        out_shape=jax.ShapeDtypeStruct((B, S, NSA_HEADS * hd), BF16),
        scratch_shapes=[pltpu.VMEM((n_slc, QB), F32),
                        pltpu.VMEM((n_cp, R * QB), F32),
                        pltpu.VMEM((WIN_KEYS, R * QB), F32),
                        pltpu.VMEM((SLC_TILE, R * QB), F32),
                        pltpu.VMEM((SLC_TILE, R * QB), F32),
                        pltpu.VMEM((SLC_TILE, R * QB), BF16),
                        pltpu.VMEM((SLC_TILE, R * QB), BF16),
                        pltpu.VMEM((hd, R * QB), F32)],
        compiler_params=_params("parallel", "parallel", "arbitrary"),
        name="nsa_attention",
    )(proj_t, proj_t, k_cmp, v_cmp_t, k_slc, v_t, k_win, v_t, c2s_t)


def _cmp_to_slc_t(n_cp, n_slc):
    i = np.arange(n_cp)[None, :] * CMP_STRIDE
    j = np.arange(n_slc)[:, None] * SLC_BLOCK
    ov = np.clip(np.minimum(i + CMP_LEN, j + SLC_BLOCK) - np.maximum(i, j), 0, None) / CMP_LEN
    ov[:, n_cp - 1] = 0.0
    return jnp.asarray(ov, dtype=BF16)


def _nsa_shared_kv(x, kv_sc, kv_sh, w_kv, cmp_pe, cmp_w1, cmp_b1, cmp_w2):
    B, S, _ = x.shape
    G, hd = NSA_GROUPS, NSA_HEAD_DIM
    kv = _modmm(x, kv_sc, kv_sh, w_kv.astype(BF16), transposed=False, tn=512)
    kv = kv.reshape(B, S, N_BRANCH, 2, G, hd)
    NP = S // CMP_STRIDE
    pieces = kv[:, :, 0].reshape(B, NP, CMP_STRIDE, 2, G, hd).transpose(3, 0, 4, 1, 2, 5)
    pieces = pieces.reshape(2, B, G, NP, CMP_STRIDE * hd)
    comp, comp_t = _compress(pieces, cmp_pe.reshape(2, 2, CMP_STRIDE * hd), cmp_w1.astype(BF16),
                             cmp_b1[:, None, :], cmp_w2.astype(BF16), cmp_w2.transpose(0, 2, 1).astype(BF16))
    to_k = lambda t: t.transpose(0, 2, 1, 3).astype(BF16)
    w_v = w_kv.reshape(-1, N_BRANCH, 2, G * hd)[:, 1:, 1].reshape(-1, 2 * G * hd)
    v_t = _modmm(x, kv_sc, kv_sh, w_v.T.astype(BF16), transposed=True, tn=2 * G * hd, out_dtype=BF16)
    return comp[0], comp_t[1], to_k(kv[:, :, 1, 0]), to_k(kv[:, :, 2, 0]), v_t


def kernel(x, c, ada_w, ada_b, ln_g, ln_b, ret_w_in, ret_w_o, kv_ada_w, kv_ada_b, nsa_w_kv, cmp_pe, cmp_w1, cmp_b1, cmp_w2, nsa_w_in, nsa_w_o, peer_w_q, peer_keys, peer_u, peer_v):
    B, S, D = x.shape
    mods = _cmod(c, ada_w, ada_b)
    kv_mod = _cmod(c, kv_ada_w[None], kv_ada_b[None])[0]
    ret_tables = _retention_tables(S, D // RET_HEADS)
    hd = NSA_HEAD_DIM
    n_gate = NSA_HEADS * N_BRANCH
    c2s_t = _cmp_to_slc_t(S // CMP_STRIDE, S // SLC_BLOCK)
    u_all = peer_u.astype(BF16)
    v_t_all = peer_v.transpose(0, 2, 1).astype(BF16)
    shared = None
    for layer in range(DEPTH):
        sh1, sc1, g1, sh2, sc2, g2 = [m[:, None, :] for m in jnp.split(mods[layer], 6, axis=-1)]
        if layer < N_A_LAYERS:
            proj = _modmm(x, sc1, sh1, ret_w_in[layer].astype(BF16), transposed=False, tn=1024, tm=1024)
            a = _retention(proj, ret_tables)
            w_o = ret_w_o[layer]
        else:
            lb = layer - N_A_LAYERS
            w_in = nsa_w_in[lb]
            w_t = jnp.concatenate([w_in[:, :NSA_HEADS * hd].T * (hd ** -0.5), w_in[:, NSA_HEADS * hd:].T,
                                   jnp.zeros((LANES - n_gate, D), F32)], axis=0).astype(BF16)
            proj_t = _modmm(x, sc1, sh1, w_t, transposed=True, tn=384)
            a = _nsa_attention(proj_t, shared, c2s_t)
            w_o = nsa_w_o[lb]
        x = _oproj_ln(a, w_o.astype(BF16), x, g1, ln_g[layer, 0], ln_b[layer, 0])
        keys = peer_keys[layer].reshape(PEER_HEADS * 2, PEER_NKEYS, -1).astype(BF16)
        x = _peer(x, sc2, sh2, g2, peer_w_q[layer].T.astype(BF16), keys, u_all, v_t_all, layer,
                  ln_g[layer, 1], ln_b[layer, 1])
        if layer == N_A_LAYERS - 1:
            kv_sh, kv_sc = [m[:, None, :] for m in jnp.split(kv_mod, 2, axis=-1)]
            shared = _nsa_shared_kv(x, kv_sc, kv_sh, nsa_w_kv, cmp_pe, cmp_w1, cmp_b1, cmp_w2)
    return x
```

```python
import functools

import numpy as np
import jax
import jax.numpy as jnp
from jax import lax
from jax.experimental import pallas as pl
from jax.experimental.pallas import tpu as pltpu

DEPTH = 4
N_A_LAYERS = DEPTH // 2
ALPHA = (2.0 * DEPTH) ** 0.25
LN_EPS = 1e-5
NEG_INF = -1e30
MASKED = 2.0 * NEG_INF
LOG2E = 1.4426950408889634

RET_HEADS = 4
RET_CHUNK = 128

NSA_HEADS = 16
NSA_GROUPS = 4
NSA_REP = NSA_HEADS // NSA_GROUPS
NSA_HEAD_DIM = 64
N_BRANCH = 3
CMP_STRIDE = 16
CMP_LEN = 32
SLC_BLOCK = 64
SLC_TOPK = 16
WINDOW = 512
Q_BLOCK = 256
FORCE_BONUS = 100.0
SLC_TILE = 512
WIN_KEYS = WINDOW + Q_BLOCK

PEER_HEADS = 8
PEER_NKEYS = 128
PEER_TOPK = 16

LANES = 128
VMEM_LIMIT = 56 * 1024 * 1024

F32 = jnp.float32
BF16 = jnp.bfloat16
_NT = (((1,), (1,)), ((), ()))


def _params(*sem):
    return pltpu.CompilerParams(dimension_semantics=sem, vmem_limit_bytes=VMEM_LIMIT)


def _split3(a):
    hi = a.astype(BF16)
    r1 = a - hi.astype(F32)
    mid = r1.astype(BF16)
    lo = (r1 - mid.astype(F32)).astype(BF16)
    return hi, mid, lo


def _layer_norm(z, g, b):
    mu = jnp.mean(z, axis=-1, keepdims=True)
    zc = z - mu
    var = jnp.mean(zc * zc, axis=-1, keepdims=True)
    return zc * lax.rsqrt(var + LN_EPS) * g + b


_GELU_K1 = -2.0 * 0.7978845608028654 * LOG2E
_GELU_K3 = _GELU_K1 * 0.044715


def _gelu_tanh(x):
    return x * (1.0 / (1.0 + jnp.exp2(x * (_GELU_K3 * (x * x) + _GELU_K1))))


def _cmod_kernel(c_ref, w_ref, b_ref, o_ref):
    c = c_ref[...]
    ca = c * jax.nn.sigmoid(c)
    w = w_ref[0]
    c_hi, c_mid, c_lo = _split3(ca)
    w_hi, w_mid, w_lo = _split3(w)
    dot = lambda a, b: jnp.dot(a, b, preferred_element_type=F32)
    acc = dot(c_hi, w_hi) + (dot(c_hi, w_mid) + dot(c_mid, w_hi)) + (dot(c_hi, w_lo) + dot(c_mid, w_mid) + dot(c_lo, w_hi))
    o_ref[0] = acc + b_ref[0]


def _cmod(c, w, b):
    L, D, N = w.shape
    n_b = c.shape[0]
    B = 16
    c = jnp.zeros((B, D), F32).at[:n_b].set(c)
    tn = 512
    out = pl.pallas_call(
        _cmod_kernel,
        grid=(L, N // tn),
        in_specs=[pl.BlockSpec((B, D), lambda l, n: (0, 0)),
                  pl.BlockSpec((1, D, tn), lambda l, n: (l, 0, n)),
                  pl.BlockSpec((1, 1, tn), lambda l, n: (l, 0, n))],
        out_specs=pl.BlockSpec((1, B, tn), lambda l, n: (l, 0, n)),
        out_shape=jax.ShapeDtypeStruct((L, B, N), F32),
        compiler_params=_params("parallel", "parallel"),
        name="cmod",
    )(c, w, b.reshape(L, 1, N))
    return out[:, :n_b]


def _modmm_kernel(x_ref, sc_ref, sh_ref, w_ref, o_ref, h_ref, *, transposed):
    @pl.when(pl.program_id(2) == 0)
    def _():
        h = x_ref[0] * (1.0 + sc_ref[0]) + sh_ref[0]
        h_ref[...] = h.astype(BF16)

    if transposed:
        o = lax.dot_general(w_ref[...], h_ref[...], _NT, preferred_element_type=F32)
    else:
        o = jnp.dot(h_ref[...], w_ref[...], preferred_element_type=F32)
    o_ref[0] = o.astype(o_ref.dtype)


def _modmm(x, sc, sh, w, *, transposed, tn, out_dtype=F32, tm=512):
    B, S, D = x.shape
    N = w.shape[0] if transposed else w.shape[1]
    if transposed:
        w_spec = pl.BlockSpec((tn, D), lambda b, s, n: (n, 0))
        o_spec = pl.BlockSpec((1, tn, tm), lambda b, s, n: (b, n, s))
        o_shape = (B, N, S)
    else:
        w_spec = pl.BlockSpec((D, tn), lambda b, s, n: (0, n))
        o_spec = pl.BlockSpec((1, tm, tn), lambda b, s, n: (b, s, n))
        o_shape = (B, S, N)
    vec = pl.BlockSpec((1, 1, D), lambda b, s, n: (b, 0, 0))
    return pl.pallas_call(
        functools.partial(_modmm_kernel, transposed=transposed),
        grid=(B, S // tm, N // tn),
        in_specs=[pl.BlockSpec((1, tm, D), lambda b, s, n: (b, s, 0)), vec, vec, w_spec],
        out_specs=o_spec,
        out_shape=jax.ShapeDtypeStruct(o_shape, out_dtype),
        scratch_shapes=[pltpu.VMEM((tm, D), BF16)],
        compiler_params=_params("parallel", "parallel", "arbitrary"),
        name="modmm_t" if transposed else "modmm",
    )(x, sc, sh, w)


def _ret_kernel(q_ref, k_ref, v_ref, g_ref, cos_ref, sin_ref, decay_ref, qdec_ref, kdec_ref, cdec_ref,
                o_ref, state_ref, *, dk, cps):
    @pl.when(pl.program_id(2) == 0)
    def _():
        state_ref[...] = jnp.zeros_like(state_ref)

    C = RET_CHUNK
    half = dk // 2
    state = state_ref[...]
    for j in range(cps):
        rows = slice(j * C, (j + 1) * C)
        cos, sin = cos_ref[rows, :], sin_ref[rows, :]

        def rot(t):
            x1, x2 = t[:, :half], t[:, half:]
            return jnp.concatenate([x1 * cos - x2 * sin, x1 * sin + x2 * cos], axis=-1)

        q = rot(q_ref[0, rows, :])
        k = rot(k_ref[0, rows, :]) * (dk ** -0.5)
        vb = v_ref[0, rows, :].astype(BF16)
        qb = q.astype(BF16)
        s = lax.dot_general(qb, k.astype(BF16), _NT, preferred_element_type=F32) * decay_ref[0]
        inner = jnp.dot(s.astype(BF16), vb, preferred_element_type=F32)
        cross = jnp.dot(qb, state.astype(BF16), preferred_element_type=F32) * qdec_ref[0]
        kd_t = (k * kdec_ref[0]).T.astype(BF16)
        state = state * cdec_ref[0] + jnp.dot(kd_t, vb, preferred_element_type=F32)
        o = inner + cross
        mu = jnp.mean(o, axis=-1, keepdims=True)
        oc = o - mu
        var = jnp.mean(oc * oc, axis=-1, keepdims=True)
        on = oc * lax.rsqrt(var + LN_EPS)
        g = g_ref[0, rows, :]
        o_ref[0, rows, :] = (g * jax.nn.sigmoid(g) * on).astype(o_ref.dtype)
    state_ref[...] = state


def _retention(proj, tables, cps=4):
    B, S, n_in = proj.shape
    H, C = RET_HEADS, RET_CHUNK
    dk = n_in // (6 * H)
    dv = 2 * dk
    cos, sin, decay, qdec, kdec, cdec = tables
    T = cps * C
    return pl.pallas_call(
        functools.partial(_ret_kernel, dk=dk, cps=cps),
        grid=(B, H, S // T),
        in_specs=[
            pl.BlockSpec((1, T, dk), lambda b, h, c: (b, c, h)),
            pl.BlockSpec((1, T, dk), lambda b, h, c: (b, c, H + h)),
            pl.BlockSpec((1, T, dv), lambda b, h, c: (b, c, H + h)),
            pl.BlockSpec((1, T, dv), lambda b, h, c: (b, c, 2 * H + h)),
            pl.BlockSpec((T, dk // 2), lambda b, h, c: (c, 0)),
            pl.BlockSpec((T, dk // 2), lambda b, h, c: (c, 0)),
            pl.BlockSpec((1, C, C), lambda b, h, c: (h, 0, 0)),
            pl.BlockSpec((1, C, dv), lambda b, h, c: (h, 0, 0)),
            pl.BlockSpec((1, C, dk), lambda b, h, c: (h, 0, 0)),
            pl.BlockSpec((1, 1, dv), lambda b, h, c: (h, 0, 0)),
        ],
        out_specs=pl.BlockSpec((1, T, dv), lambda b, h, c: (b, c, h)),
        out_shape=jax.ShapeDtypeStruct((B, S, H * dv), BF16),
        scratch_shapes=[pltpu.VMEM((dk, dv), F32)],
        compiler_params=_params("parallel", "parallel", "arbitrary"),
        name="retention",
    )(proj, proj, proj, proj, cos, sin, decay, qdec, kdec, cdec)


def _retention_tables(S, dk):
    H, C = RET_HEADS, RET_CHUNK
    dv = 2 * dk
    pos = jnp.arange(S, dtype=F32)
    theta = 1.0 / (10000.0 ** jnp.linspace(0.0, 1.0, dk // 2, dtype=F32))
    ang = pos[:, None] * theta[None, :]
    log_g = jnp.log1p(-jnp.exp2(-5.0 - jnp.arange(H, dtype=F32)))
    idx = jnp.arange(C, dtype=F32)
    diff = idx[:, None] - idx[None, :]
    decay = jnp.where(diff >= 0, jnp.exp(jnp.maximum(diff, 0.0)[None] * log_g[:, None, None]), 0.0)
    q_dec = jnp.exp((idx + 1.0)[None] * log_g[:, None])
    k_dec = jnp.exp((C - 1.0 - idx)[None] * log_g[:, None])
    c_dec = jnp.exp(C * log_g)
    return (jnp.cos(ang), jnp.sin(ang), decay,
            jnp.broadcast_to(q_dec[:, :, None], (H, C, dv)),
            jnp.broadcast_to(k_dec[:, :, None], (H, C, dk)),
            jnp.broadcast_to(c_dec[:, None, None], (H, 1, dv)))


def _oproj_ln_kernel(a_ref, w_ref, x_ref, g_ref, lg_ref, lb_ref, o_ref):
    y = jnp.dot(a_ref[0], w_ref[...], preferred_element_type=F32)
    z = ALPHA * x_ref[0] + g_ref[0] * y
    o_ref[0] = _layer_norm(z, lg_ref[...], lb_ref[...])


def _oproj_ln(a, w, x, gate, lg, lb, tm=512):
    B, S, D = x.shape
    K = a.shape[-1]
    vec = pl.BlockSpec((1, 1, D), lambda b, s: (b, 0, 0))
    par = pl.BlockSpec((1, D), lambda b, s: (0, 0))
    return pl.pallas_call(
        _oproj_ln_kernel,
        grid=(B, S // tm),
        in_specs=[pl.BlockSpec((1, tm, K), lambda b, s: (b, s, 0)),
                  pl.BlockSpec((K, D), lambda b, s: (0, 0)),
                  pl.BlockSpec((1, tm, D), lambda b, s: (b, s, 0)),
                  vec, par, par],
        out_specs=pl.BlockSpec((1, tm, D), lambda b, s: (b, s, 0)),
        out_shape=jax.ShapeDtypeStruct((B, S, D), F32),
        compiler_params=_params("parallel", "parallel"),
        name="oproj_ln",
    )(a, w, x, gate, lg.reshape(1, D), lb.reshape(1, D))


def _top_values(s, k):
    vals = []
    for _ in range(k):
        m = jnp.max(s, axis=0, keepdims=True)
        vals.append(m)
        s = jnp.where(s == m, -jnp.inf, s)
    return vals


def _top_values_ranked(s, k):
    vals = []
    rank = jnp.full(s.shape, float(k), F32)
    for i in range(k):
        m = jnp.max(s, axis=0, keepdims=True)
        vals.append(m)
        hit = s == m
        rank = jnp.where(hit, float(i), rank)
        s = jnp.where(hit, -jnp.inf, s)
    return vals, rank


def _twice_bf16(x):
    b = pltpu.bitcast(x, jnp.uint32)
    return b | (b >> 16)


_PEER_PAIRS = [(p, q) for p in range(PEER_TOPK) for q in range(PEER_TOPK) if (p + 1) * (q + 1) <= PEER_TOPK + 1]
_PEER_CAND_ROWS = -(-len(_PEER_PAIRS) // 8) * 8


def _peer_kernel(x_ref, sc_ref, sh_ref, g_ref, wq_ref, keys_ref, u_ref, vt_ref, lg_ref, lb_ref, o_ref,
                 h_ref, w1w_ref, lw_ref, w2b_ref, r2b_ref, cand_ref, a_ref, ws_ref, p_ref, acc_ref, *, te, sub, kchunk, wb):
    e = pl.program_id(2)
    tm = h_ref.shape[0]

    @pl.when(e == 0)
    def _route():
        h = (x_ref[0] * (1.0 + sc_ref[0]) + sh_ref[0]).astype(BF16)
        h_ref[...] = h
        q_t = lax.dot_general(wq_ref[...], h, _NT, preferred_element_type=F32).astype(BF16)
        dq = keys_ref.shape[2]
        cand_ref[...] = jnp.full(cand_ref.shape, -jnp.inf, F32)
        for hd in range(PEER_HEADS):
            s1_all = jnp.dot(keys_ref[2 * hd], q_t[(2 * hd) * dq:(2 * hd + 1) * dq], preferred_element_type=F32)
            s2_all = jnp.dot(keys_ref[2 * hd + 1], q_t[(2 * hd + 1) * dq:(2 * hd + 2) * dq], preferred_element_type=F32)
            for lt in range(tm // LANES):
                ln = slice(lt * LANES, (lt + 1) * LANES)
                s1, s2 = s1_all[:, ln], s2_all[:, ln]
                a1 = _top_values(s1, PEER_TOPK)
                a2, rank2 = _top_values_ranked(s2, PEER_TOPK)
                for i, (p, q) in enumerate(_PEER_PAIRS):
                    cand_ref[i:i + 1, ln] = a1[p] + a2[q]
                c = _top_values(cand_ref[:, ln], PEER_TOPK + 1)
                z = jnp.ones_like(c[0])
                for kk in range(1, PEER_TOPK):
                    z = z + jnp.exp(c[kk] - c[0])
                inv_z = 1.0 / z
                tau = 0.5 * (c[PEER_TOPK - 1] + c[PEER_TOPK])
                n_ok = jnp.zeros_like(s1)
                for q in range(PEER_TOPK):
                    n_ok = jnp.where(s1 >= tau - a2[q], float(q + 1), n_ok)
                w1 = (jnp.exp(s1 - a1[0]) * inv_z).astype(BF16).astype(F32)
                by_step = lambda t: t.reshape(w1w_ref.shape[1], w1w_ref.shape[2], LANES)
                w1w_ref[hd, :, :, ln] = by_step(_twice_bf16(w1))
                lw_ref[hd, :, :, ln] = by_step(_twice_bf16(n_ok))
                w2b_ref[hd, :, ln] = jnp.exp(s2 - a2[0]).astype(BF16)
                r2b_ref[hd, :, ln] = rank2.astype(BF16)
        acc_ref[...] = jnp.zeros_like(acc_ref)

    per_sub = sub // PEER_NKEYS
    n_sub = te // sub

    def activations(sb):
        a_ref[sb] = lax.dot_general(u_ref[sb * sub:(sb + 1) * sub, :], h_ref[...], _NT,
                                    preferred_element_type=F32)

    activations(0)
    for sb in range(n_sub):
        for ii in range(per_sub):
            k = sb * per_sub + ii
            for lt in range(tm // wb):
                ln = slice(lt * wb, (lt + 1) * wb)
                wsum = None
                for hd in range(PEER_HEADS):
                    row_bf16 = lambda ref: pltpu.bitcast(
                        jnp.broadcast_to(ref[hd, e, k:k + 1, ln], (PEER_NKEYS // 2, wb)), BF16)
                    keep = r2b_ref[hd, :, ln] < row_bf16(lw_ref)
                    term = jnp.where(keep, w2b_ref[hd, :, ln], jnp.zeros((), BF16)) * row_bf16(w1w_ref)
                    wsum = term if wsum is None else wsum + term
                ws_ref[ii * PEER_NKEYS:(ii + 1) * PEER_NKEYS, ln] = wsum
        if sb + 1 < n_sub:
            activations(sb + 1)
        for ii in range(per_sub):
            rows = slice(ii * PEER_NKEYS, (ii + 1) * PEER_NKEYS)
            p_ref[pl.ds(sb * sub + ii * PEER_NKEYS, PEER_NKEYS), :] = (
                _gelu_tanh(a_ref[sb, rows, :]).astype(BF16) * ws_ref[rows, :])
        done = (sb + 1) * sub
        if done % kchunk == 0:
            cols = slice(done - kchunk, done)
            acc_ref[...] += jnp.dot(vt_ref[:, cols], p_ref[cols, :], preferred_element_type=F32)

    @pl.when(e == pl.num_programs(2) - 1)
    def _finish():
        y = acc_ref[...].T
        z = ALPHA * x_ref[0] + g_ref[0] * y
        o_ref[0] = _layer_norm(z, lg_ref[...], lb_ref[...])


def _peer(x, sc, sh, gate, wq_t, keys, u, v_t, layer, lg, lb, tm=512, te=2048, sub=256, kchunk=1024, wb=256):
    B, S, D = x.shape
    E = u.shape[1]
    nq = wq_t.shape[0]
    vec = pl.BlockSpec((1, 1, D), lambda b, s, e: (b, 0, 0))
    par = pl.BlockSpec((1, D), lambda b, s, e: (0, 0))
    return pl.pallas_call(
        functools.partial(_peer_kernel, te=te, sub=sub, kchunk=kchunk, wb=wb),
        grid=(B, S // tm, E // te),
        in_specs=[pl.BlockSpec((1, tm, D), lambda b, s, e: (b, s, 0)), vec, vec, vec,
                  pl.BlockSpec((nq, D), lambda b, s, e: (0, 0), pipeline_mode=pl.Buffered(1)),
                  pl.BlockSpec(keys.shape, lambda b, s, e: (0, 0, 0), pipeline_mode=pl.Buffered(1)),
                  pl.BlockSpec((None, te, D), lambda b, s, e: (layer, e, 0)),
                  pl.BlockSpec((None, D, te), lambda b, s, e: (layer, 0, e)),
                  par, par],
        out_specs=pl.BlockSpec((1, tm, D), lambda b, s, e: (b, s, 0)),
        out_shape=jax.ShapeDtypeStruct((B, S, D), F32),
        scratch_shapes=[pltpu.VMEM((tm, D), BF16),
                        pltpu.VMEM((PEER_HEADS, E // te, te // PEER_NKEYS, tm), jnp.uint32),
                        pltpu.VMEM((PEER_HEADS, E // te, te // PEER_NKEYS, tm), jnp.uint32),
                        pltpu.VMEM((PEER_HEADS, PEER_NKEYS, tm), BF16),
                        pltpu.VMEM((PEER_HEADS, PEER_NKEYS, tm), BF16),
                        pltpu.VMEM((_PEER_CAND_ROWS, tm), F32),
                        pltpu.VMEM((te // sub, sub, tm), F32),
                        pltpu.VMEM((sub, tm), BF16),
                        pltpu.VMEM((te, tm), BF16),
                        pltpu.VMEM((D, tm), F32)],
        compiler_params=_params("parallel", "parallel", "arbitrary"),
        name="peer",
    )(x, sc, sh, gate, wq_t, keys, u, v_t, lg.reshape(1, D), lb.reshape(1, D))


def _compress_kernel(pc_ref, pe_ref, w1_ref, b1_ref, w2_ref, w2t_ref, o_ref, ot_ref):
    pc = pc_ref[0, 0, 0]
    half = pc.shape[1]
    lo = (pc + pe_ref[0, 0:1, :]).astype(BF16)
    hi = (pc + pe_ref[0, 1:2, :]).astype(BF16)
    a = jnp.dot(lo, w1_ref[0, :half, :], preferred_element_type=F32)
    b = jnp.dot(hi, w1_ref[0, half:, :], preferred_element_type=F32)
    b_next = pltpu.roll(b, pc.shape[0] - 1, 0)
    hid = jax.nn.gelu(a + b_next + b1_ref[0]).astype(BF16)
    o_ref[0, 0, 0] = jnp.dot(hid, w2_ref[0], preferred_element_type=F32).astype(o_ref.dtype)
    ot_ref[0, 0, 0] = lax.dot_general(w2t_ref[0], hid, _NT, preferred_element_type=F32).astype(ot_ref.dtype)


def _compress(pieces, pe2, w1, b1, w2, w2t):
    _, B, G, NP, F = pieces.shape
    Hd = w1.shape[2]
    hd = w2.shape[2]
    return pl.pallas_call(
        _compress_kernel,
        grid=(2, B, G),
        in_specs=[pl.BlockSpec((1, 1, 1, NP, F), lambda c, b, g: (c, b, g, 0, 0)),
                  pl.BlockSpec((1, 2, F), lambda c, b, g: (c, 0, 0)),
                  pl.BlockSpec((1, 2 * F, Hd), lambda c, b, g: (c, 0, 0)),
                  pl.BlockSpec((1, 1, Hd), lambda c, b, g: (c, 0, 0)),
                  pl.BlockSpec((1, Hd, hd), lambda c, b, g: (c, 0, 0)),
                  pl.BlockSpec((1, hd, Hd), lambda c, b, g: (c, 0, 0))],
        out_specs=[pl.BlockSpec((1, 1, 1, NP, hd), lambda c, b, g: (c, b, g, 0, 0)),
                   pl.BlockSpec((1, 1, 1, hd, NP), lambda c, b, g: (c, b, g, 0, 0))],
        out_shape=[jax.ShapeDtypeStruct((2, B, G, NP, hd), BF16),
                   jax.ShapeDtypeStruct((2, B, G, hd, NP), BF16)],
        compiler_params=_params("parallel", "parallel", "parallel"),
        name="compress",
    )(pieces, pe2, w1, b1, w2, w2t)


def _softmax_t(s, mask):
    sm = jnp.where(mask, s, MASKED)
    m = jnp.maximum(jnp.max(sm, axis=0, keepdims=True), NEG_INF)
    e = jnp.exp2(sm - m)
    return e, 1.0 / jnp.maximum(jnp.sum(e, axis=0, keepdims=True), 1e-30)


def _nsa_kernel(qt_ref, gl_ref, kc_ref, vct_ref, ks_ref, vst_ref, kw_ref, vwt_ref, c2s_ref, o_ref,
                sel_ref, sc_ref, sw_ref, sa_ref, sb_ref, pa_ref, pb_ref, acc_ref):
    R, hd, QB = NSA_REP, NSA_HEAD_DIM, Q_BLOCK
    g = pl.program_id(1)
    start = pl.program_id(2) * QB
    q4 = qt_ref[0] * LOG2E
    q_t = jnp.concatenate([q4[r * hd:(r + 1) * hd] for r in range(R)], axis=1).astype(BF16)
    t_q = start + lax.broadcasted_iota(jnp.int32, (1, QB), 1)
    lanes = lambda r: slice(r * QB, (r + 1) * QB)
    dot = lambda a, b: jnp.dot(a, b, preferred_element_type=F32)
    n_tiles = ks_ref.shape[2] // SLC_TILE

    def k_slc_tile(kt):
        return ks_ref[0, 0, pl.ds(pl.multiple_of(kt * SLC_TILE, SLC_TILE), SLC_TILE), :]

    base = pl.multiple_of(jnp.maximum(start - WINDOW, 0), QB)
    sc_ref[...] = dot(kc_ref[0, 0], q_t)
    sw_ref[...] = dot(kw_ref[0, 0, pl.ds(base, WIN_KEYS), :], q_t)
    sa_ref[...] = dot(k_slc_tile(0), q_t)

    n_cp = kc_ref.shape[2]
    n_idx = lax.broadcasted_iota(jnp.int32, (n_cp, 1), 0)
    cmask = (n_idx * CMP_STRIDE + (CMP_LEN - 1)) <= t_q
    v_cmp_t = vct_ref[0, 0]
    o_cmp, psum = [], None
    for r in range(R):
        e, inv = _softmax_t(sc_ref[:, lanes(r)], cmask)
        p = e * inv
        o_cmp.append(dot(v_cmp_t, p.astype(BF16)))
        psum = p if psum is None else psum + p

    kp = base + lax.broadcasted_iota(jnp.int32, (WIN_KEYS, 1), 0)
    wmask = (kp <= t_q) & (kp > t_q - WINDOW)

    def window_head(r):
        e, inv = _softmax_t(sw_ref[:, lanes(r)], wmask)
        return dot(vwt_ref[0, :, pl.ds(base, WIN_KEYS)], e.astype(BF16)) * inv

    c2s = c2s_ref[...]
    imp = None
    for part in _split3(psum):
        d = dot(c2s, part)
        imp = d if imp is None else imp + d
    n_slc = c2s.shape[0]
    j_idx = lax.broadcasted_iota(jnp.int32, (n_slc, 1), 0)
    cur = jnp.right_shift(t_q, SLC_BLOCK.bit_length() - 1)
    forced = (j_idx == 0) | (j_idx == cur) | (j_idx == cur - 1)
    avail = (j_idx * SLC_BLOCK) <= t_q
    score = jnp.where(avail, imp + jnp.where(forced, FORCE_BONUS, 0.0), -1.0)
    work = score
    sel = jnp.zeros(score.shape, F32)
    n_sel = min(SLC_TOPK, n_slc)
    o_win = []
    for it in range(n_sel):
        mx = jnp.max(work, axis=0, keepdims=True)
        first = jnp.min(jnp.where(work == mx, j_idx, n_slc), axis=0, keepdims=True)
        hit = j_idx == first
        sel = jnp.where(hit, 1.0, sel)
        work = jnp.where(hit, -jnp.inf, work)
        if (it + 1) % max(n_sel // R, 1) == 0 and len(o_win) < R:
            o_win.append(window_head(len(o_win)))
    o_win += [window_head(r) for r in range(len(o_win), R)]
    sel_ref[...] = jnp.where(score >= 0.0, sel, 0.0)

    per_tile = SLC_TILE // SLC_BLOCK
    row_tok = lax.broadcasted_iota(jnp.int32, (SLC_TILE, 1), 0)
    last = start // SLC_TILE

    def scores(kt):
        return dot(k_slc_tile(jnp.minimum(kt, n_tiles - 1)), q_t)

    def softmax_update(s_buf, p_buf, kt, causal, m, l):
        blk = pl.multiple_of(jnp.minimum(kt, n_tiles - 1) * per_tile, per_tile)
        sel_rows = sel_ref[pl.ds(blk, per_tile), :]
        mask = jnp.concatenate(
            [jnp.broadcast_to(sel_rows[a:a + 1, :], (SLC_BLOCK, QB)) for a in range(per_tile)], axis=0) > 0.5
        if causal:
            mask = mask & ((kt * SLC_TILE + row_tok) <= t_q)
        m_out, l_out, alphas = [], [], []
        for r in range(R):
            sm = jnp.where(mask, s_buf[:, lanes(r)], MASKED)
            m_new = jnp.maximum(m[r], jnp.max(sm, axis=0, keepdims=True))
            p = jnp.exp2(sm - m_new)
            alphas.append(jnp.exp2(m[r] - m_new))
            p_buf[:, lanes(r)] = p.astype(BF16)
            m_out.append(m_new)
            l_out.append(l[r] * alphas[r] + jnp.sum(p, axis=0, keepdims=True))
        return tuple(m_out), tuple(l_out), jnp.concatenate(alphas, axis=1)

    def v_tile_t(kt):
        k0 = pl.multiple_of(jnp.clip(kt, 0, n_tiles - 1) * SLC_TILE, SLC_TILE)
        return vst_ref[0, :, pl.ds(k0, SLC_TILE)]

    def pair_step(t0, carry, causal):
        m, l, alpha_prev = carry
        m, l, alpha0 = softmax_update(sa_ref, pa_ref, t0, causal, m, l)
        sb_ref[...] = scores(t0 + 1)
        acc_ref[...] = acc_ref[...] * alpha_prev + dot(v_tile_t(t0 - 1), pb_ref[...])
        m, l, alpha1 = softmax_update(sb_ref, pb_ref, t0 + 1, causal, m, l)
        sa_ref[...] = scores(t0 + 2)
        acc_ref[...] = acc_ref[...] * alpha0 + dot(v_tile_t(t0), pa_ref[...])
        return m, l, alpha1

    pb_ref[...] = jnp.zeros_like(pb_ref)
    acc_ref[...] = jnp.zeros_like(acc_ref)
    init = (tuple(jnp.full((1, QB), NEG_INF, F32) for _ in range(R)),
            tuple(jnp.zeros((1, QB), F32) for _ in range(R)),
            jnp.ones((1, R * QB), F32))
    n_pairs = last // 2
    carry = lax.fori_loop(0, n_pairs, lambda i, c: pair_step(2 * i, c, False), init)
    _, l, alpha = pair_step(2 * n_pairs, carry, True)
    acc = acc_ref[...] * alpha + dot(v_tile_t(2 * n_pairs + 1), pb_ref[...])
    o_slc = [acc[:, lanes(r)] * (1.0 / jnp.maximum(l[r], 1e-30)) for r in range(R)]

    outs = []
    for r in range(R):
        row = (g * R + r) * N_BRANCH
        gate = [jax.nn.sigmoid(gl_ref[0, pl.ds(row + br, 1), :]) for br in range(N_BRANCH)]
        outs.append(gate[0] * o_cmp[r] + gate[1] * o_slc[r] + gate[2] * o_win[r])
    o_ref[0] = jnp.concatenate(outs, axis=0).T.astype(o_ref.dtype)


def _nsa_attention(proj_t, shared, c2s_t):
    k_cmp, v_cmp_t, k_slc, k_win, v_t = shared
    B, _, S = proj_t.shape
    G, R, hd, QB = NSA_GROUPS, NSA_REP, NSA_HEAD_DIM, Q_BLOCK
    n_cp = k_cmp.shape[2]
    n_slc = S // SLC_BLOCK
    gate_blk = (NSA_HEADS * hd) // LANES
    whole = lambda shape: pl.BlockSpec((1, 1) + shape, lambda b, g, q: (b, g) + (0,) * len(shape))
    v_rows = lambda branch: pl.BlockSpec((1, hd, S), lambda b, g, q: (b, branch * G + g, 0))
    return pl.pallas_call(
        _nsa_kernel,
        grid=(B, G, S // QB),
        in_specs=[pl.BlockSpec((1, R * hd, QB), lambda b, g, q: (b, g, q)),
                  pl.BlockSpec((1, LANES, QB), lambda b, g, q: (b, gate_blk, q)),
                  whole((n_cp, hd)), whole((hd, n_cp)),
                  whole((S, hd)), v_rows(0),
                  whole((S, hd)), v_rows(1),
                  pl.BlockSpec((n_slc, n_cp), lambda b, g, q: (0, 0))],
        out_specs=pl.BlockSpec((1, QB, R * hd), lambda b, g, q: (b, q, g)),
        out_shape=jax.ShapeDtypeStruct((B, S, NSA_HEADS * hd), BF16),
        scratch_shapes=[pltpu.VMEM((n_slc, QB), F32),
                        pltpu.VMEM((n_cp, R * QB), F32),
                        pltpu.VMEM((WIN_KEYS, R * QB), F32),
                        pltpu.VMEM((SLC_TILE, R * QB), F32),
                        pltpu.VMEM((SLC_TILE, R * QB), F32),
                        pltpu.VMEM((SLC_TILE, R * QB), BF16),
                        pltpu.VMEM((SLC_TILE, R * QB), BF16),
                        pltpu.VMEM((hd, R * QB), F32)],
        compiler_params=_params("parallel", "parallel", "arbitrary"),
        name="nsa_attention",
    )(proj_t, proj_t, k_cmp, v_cmp_t, k_slc, v_t, k_win, v_t, c2s_t)


def _cmp_to_slc_t(n_cp, n_slc):
    i = np.arange(n_cp)[None, :] * CMP_STRIDE
    j = np.arange(n_slc)[:, None] * SLC_BLOCK
    ov = np.clip(np.minimum(i + CMP_LEN, j + SLC_BLOCK) - np.maximum(i, j), 0, None) / CMP_LEN
    ov[:, n_cp - 1] = 0.0
    return jnp.asarray(ov, dtype=BF16)


def _nsa_shared_kv(x, kv_sc, kv_sh, w_kv, cmp_pe, cmp_w1, cmp_b1, cmp_w2):
    B, S, _ = x.shape
    G, hd = NSA_GROUPS, NSA_HEAD_DIM
    kv = _modmm(x, kv_sc, kv_sh, w_kv.astype(BF16), transposed=False, tn=512)
    kv = kv.reshape(B, S, N_BRANCH, 2, G, hd)
    NP = S // CMP_STRIDE
    pieces = kv[:, :, 0].reshape(B, NP, CMP_STRIDE, 2, G, hd).transpose(3, 0, 4, 1, 2, 5)
    pieces = pieces.reshape(2, B, G, NP, CMP_STRIDE * hd)
    comp, comp_t = _compress(pieces, cmp_pe.reshape(2, 2, CMP_STRIDE * hd), cmp_w1.astype(BF16),
                             cmp_b1[:, None, :], cmp_w2.astype(BF16), cmp_w2.transpose(0, 2, 1).astype(BF16))
    to_k = lambda t: t.transpose(0, 2, 1, 3).astype(BF16)
    w_v = w_kv.reshape(-1, N_BRANCH, 2, G * hd)[:, 1:, 1].reshape(-1, 2 * G * hd)
    v_t = _modmm(x, kv_sc, kv_sh, w_v.T.astype(BF16), transposed=True, tn=2 * G * hd, out_dtype=BF16)
    return comp[0], comp_t[1], to_k(kv[:, :, 1, 0]), to_k(kv[:, :, 2, 0]), v_t


def kernel(x, c, ada_w, ada_b, ln_g, ln_b, ret_w_in, ret_w_o, kv_ada_w, kv_ada_b, nsa_w_kv, cmp_pe, cmp_w1, cmp_b1, cmp_w2, nsa_w_in, nsa_w_o, peer_w_q, peer_keys, peer_u, peer_v):
    B, S, D = x.shape
    mods = _cmod(c, ada_w, ada_b)
    kv_mod = _cmod(c, kv_ada_w[None], kv_ada_b[None])[0]
    ret_tables = _retention_tables(S, D // RET_HEADS)
    hd = NSA_HEAD_DIM
    n_gate = NSA_HEADS * N_BRANCH
    c2s_t = _cmp_to_slc_t(S // CMP_STRIDE, S // SLC_BLOCK)
    u_all = peer_u.astype(BF16)
    v_t_all = peer_v.transpose(0, 2, 1).astype(BF16)
    shared = None
    for layer in range(DEPTH):
        sh1, sc1, g1, sh2, sc2, g2 = [m[:, None, :] for m in jnp.split(mods[layer], 6, axis=-1)]
        if layer < N_A_LAYERS:
            proj = _modmm(x, sc1, sh1, ret_w_in[layer].astype(BF16), transposed=False, tn=1024, tm=1024)
            a = _retention(proj, ret_tables)
            w_o = ret_w_o[layer]
        else:
            lb = layer - N_A_LAYERS
            w_in = nsa_w_in[lb]
            w_t = jnp.concatenate([w_in[:, :NSA_HEADS * hd].T * (hd ** -0.5), w_in[:, NSA_HEADS * hd:].T,
                                   jnp.zeros((LANES - n_gate, D), F32)], axis=0).astype(BF16)
            proj_t = _modmm(x, sc1, sh1, w_t, transposed=True, tn=384)
            a = _nsa_attention(proj_t, shared, c2s_t)
            w_o = nsa_w_o[lb]
        x = _oproj_ln(a, w_o.astype(BF16), x, g1, ln_g[layer, 0], ln_b[layer, 0])
        keys = peer_keys[layer].reshape(PEER_HEADS * 2, PEER_NKEYS, -1).astype(BF16)
        x = _peer(x, sc2, sh2, g2, peer_w_q[layer].T.astype(BF16), keys, u_all, v_t_all, layer,
                  ln_g[layer, 1], ln_b[layer, 1])
        if layer == N_A_LAYERS - 1:
            kv_sh, kv_sc = [m[:, None, :] for m in jnp.split(kv_mod, 2, axis=-1)]
            shared = _nsa_shared_kv(x, kv_sc, kv_sh, nsa_w_kv, cmp_pe, cmp_w1, cmp_b1, cmp_w2)
    return x
```

```python
import functools

import numpy as np
import jax
import jax.numpy as jnp
from jax import lax
from jax.experimental import pallas as pl
from jax.experimental.pallas import tpu as pltpu

DEPTH = 4
N_A_LAYERS = DEPTH // 2
ALPHA = (2.0 * DEPTH) ** 0.25
LN_EPS = 1e-5
NEG_INF = -1e30
MASKED = 2.0 * NEG_INF
LOG2E = 1.4426950408889634

RET_HEADS = 4
RET_CHUNK = 128

NSA_HEADS = 16
NSA_GROUPS = 4
NSA_REP = NSA_HEADS // NSA_GROUPS
NSA_HEAD_DIM = 64
N_BRANCH = 3
CMP_STRIDE = 16
CMP_LEN = 32
SLC_BLOCK = 64
SLC_TOPK = 16
WINDOW = 512
Q_BLOCK = 512
FORCE_BONUS = 100.0
SLC_TILE = 512
WIN_KEYS = WINDOW + Q_BLOCK

PEER_HEADS = 8
PEER_NKEYS = 128
PEER_TOPK = 16

LANES = 128
VMEM_LIMIT = 56 * 1024 * 1024

F32 = jnp.float32
BF16 = jnp.bfloat16
_NT = (((1,), (1,)), ((), ()))


def _params(*sem):
    return pltpu.CompilerParams(dimension_semantics=sem, vmem_limit_bytes=VMEM_LIMIT)


def _split3(a):
    hi = a.astype(BF16)
    r1 = a - hi.astype(F32)
    mid = r1.astype(BF16)
    lo = (r1 - mid.astype(F32)).astype(BF16)
    return hi, mid, lo


def _layer_norm(z, g, b):
    mu = jnp.mean(z, axis=-1, keepdims=True)
    zc = z - mu
    var = jnp.mean(zc * zc, axis=-1, keepdims=True)
    return zc * lax.rsqrt(var + LN_EPS) * g + b


_GELU_K1 = -2.0 * 0.7978845608028654 * LOG2E
_GELU_K3 = _GELU_K1 * 0.044715


def _gelu_tanh(x):
    return x * (1.0 / (1.0 + jnp.exp2(x * (_GELU_K3 * (x * x) + _GELU_K1))))


def _cmod_kernel(c_ref, w_ref, b_ref, o_ref):
    c = c_ref[...]
    ca = c * jax.nn.sigmoid(c)
    w = w_ref[0]
    c_hi, c_mid, c_lo = _split3(ca)
    w_hi, w_mid, w_lo = _split3(w)
    dot = lambda a, b: jnp.dot(a, b, preferred_element_type=F32)
    acc = dot(c_hi, w_hi) + (dot(c_hi, w_mid) + dot(c_mid, w_hi)) + (dot(c_hi, w_lo) + dot(c_mid, w_mid) + dot(c_lo, w_hi))
    o_ref[0] = acc + b_ref[0]


def _cmod(c, w, b):
    L, D, N = w.shape
    n_b = c.shape[0]
    B = 16
    c = jnp.zeros((B, D), F32).at[:n_b].set(c)
    tn = 512
    out = pl.pallas_call(
        _cmod_kernel,
        grid=(L, N // tn),
        in_specs=[pl.BlockSpec((B, D), lambda l, n: (0, 0)),
                  pl.BlockSpec((1, D, tn), lambda l, n: (l, 0, n)),
                  pl.BlockSpec((1, 1, tn), lambda l, n: (l, 0, n))],
        out_specs=pl.BlockSpec((1, B, tn), lambda l, n: (l, 0, n)),
        out_shape=jax.ShapeDtypeStruct((L, B, N), F32),
        compiler_params=_params("parallel", "parallel"),
        name="cmod",
    )(c, w, b.reshape(L, 1, N))
    return out[:, :n_b]


def _modmm_kernel(x_ref, sc_ref, sh_ref, w_ref, o_ref, h_ref, *, transposed):
    @pl.when(pl.program_id(2) == 0)
    def _():
        h = x_ref[0] * (1.0 + sc_ref[0]) + sh_ref[0]
        h_ref[...] = h.astype(BF16)

    if transposed:
        o = lax.dot_general(w_ref[...], h_ref[...], _NT, preferred_element_type=F32)
    else:
        o = jnp.dot(h_ref[...], w_ref[...], preferred_element_type=F32)
    o_ref[0] = o.astype(o_ref.dtype)


def _modmm(x, sc, sh, w, *, transposed, tn, out_dtype=F32, tm=512):
    B, S, D = x.shape
    N = w.shape[0] if transposed else w.shape[1]
    if transposed:
        w_spec = pl.BlockSpec((tn, D), lambda b, s, n: (n, 0))
        o_spec = pl.BlockSpec((1, tn, tm), lambda b, s, n: (b, n, s))
        o_shape = (B, N, S)
    else:
        w_spec = pl.BlockSpec((D, tn), lambda b, s, n: (0, n))
        o_spec = pl.BlockSpec((1, tm, tn), lambda b, s, n: (b, s, n))
        o_shape = (B, S, N)
    vec = pl.BlockSpec((1, 1, D), lambda b, s, n: (b, 0, 0))
    return pl.pallas_call(
        functools.partial(_modmm_kernel, transposed=transposed),
        grid=(B, S // tm, N // tn),
        in_specs=[pl.BlockSpec((1, tm, D), lambda b, s, n: (b, s, 0)), vec, vec, w_spec],
        out_specs=o_spec,
        out_shape=jax.ShapeDtypeStruct(o_shape, out_dtype),
        scratch_shapes=[pltpu.VMEM((tm, D), BF16)],
        compiler_params=_params("parallel", "parallel", "arbitrary"),
        name="modmm_t" if transposed else "modmm",
    )(x, sc, sh, w)


def _ret_kernel(q_ref, k_ref, v_ref, g_ref, cos_ref, sin_ref, decay_ref, qdec_ref, kdec_ref, cdec_ref,
                o_ref, state_ref, *, dk, cps):
    @pl.when(pl.program_id(2) == 0)
    def _():
        state_ref[...] = jnp.zeros_like(state_ref)

    C = RET_CHUNK
    half = dk // 2
    state = state_ref[...]
    for j in range(cps):
        rows = slice(j * C, (j + 1) * C)
        cos, sin = cos_ref[rows, :], sin_ref[rows, :]

        def rot(t):
            x1, x2 = t[:, :half], t[:, half:]
            return jnp.concatenate([x1 * cos - x2 * sin, x1 * sin + x2 * cos], axis=-1)

        q = rot(q_ref[0, rows, :])
        k = rot(k_ref[0, rows, :]) * (dk ** -0.5)
        vb = v_ref[0, rows, :].astype(BF16)
        qb = q.astype(BF16)
        s = lax.dot_general(qb, k.astype(BF16), _NT, preferred_element_type=F32) * decay_ref[0]
        inner = jnp.dot(s.astype(BF16), vb, preferred_element_type=F32)
        cross = jnp.dot(qb, state.astype(BF16), preferred_element_type=F32) * qdec_ref[0]
        kd_t = (k * kdec_ref[0]).T.astype(BF16)
        state = state * cdec_ref[0] + jnp.dot(kd_t, vb, preferred_element_type=F32)
        o = inner + cross
        mu = jnp.mean(o, axis=-1, keepdims=True)
        oc = o - mu
        var = jnp.mean(oc * oc, axis=-1, keepdims=True)
        on = oc * lax.rsqrt(var + LN_EPS)
        g = g_ref[0, rows, :]
        o_ref[0, rows, :] = (g * jax.nn.sigmoid(g) * on).astype(o_ref.dtype)
    state_ref[...] = state


def _retention(proj, tables, cps=4):
    B, S, n_in = proj.shape
    H, C = RET_HEADS, RET_CHUNK
    dk = n_in // (6 * H)
    dv = 2 * dk
    cos, sin, decay, qdec, kdec, cdec = tables
    T = cps * C
    return pl.pallas_call(
        functools.partial(_ret_kernel, dk=dk, cps=cps),
        grid=(B, H, S // T),
        in_specs=[
            pl.BlockSpec((1, T, dk), lambda b, h, c: (b, c, h)),
            pl.BlockSpec((1, T, dk), lambda b, h, c: (b, c, H + h)),
            pl.BlockSpec((1, T, dv), lambda b, h, c: (b, c, H + h)),
            pl.BlockSpec((1, T, dv), lambda b, h, c: (b, c, 2 * H + h)),
            pl.BlockSpec((T, dk // 2), lambda b, h, c: (c, 0)),
            pl.BlockSpec((T, dk // 2), lambda b, h, c: (c, 0)),
            pl.BlockSpec((1, C, C), lambda b, h, c: (h, 0, 0)),
            pl.BlockSpec((1, C, dv), lambda b, h, c: (h, 0, 0)),
            pl.BlockSpec((1, C, dk), lambda b, h, c: (h, 0, 0)),
            pl.BlockSpec((1, 1, dv), lambda b, h, c: (h, 0, 0)),
        ],
        out_specs=pl.BlockSpec((1, T, dv), lambda b, h, c: (b, c, h)),
        out_shape=jax.ShapeDtypeStruct((B, S, H * dv), BF16),
        scratch_shapes=[pltpu.VMEM((dk, dv), F32)],
        compiler_params=_params("parallel", "parallel", "arbitrary"),
        name="retention",
    )(proj, proj, proj, proj, cos, sin, decay, qdec, kdec, cdec)


def _retention_tables(S, dk):
    H, C = RET_HEADS, RET_CHUNK
    dv = 2 * dk
    pos = jnp.arange(S, dtype=F32)
    theta = 1.0 / (10000.0 ** jnp.linspace(0.0, 1.0, dk // 2, dtype=F32))
    ang = pos[:, None] * theta[None, :]
    log_g = jnp.log1p(-jnp.exp2(-5.0 - jnp.arange(H, dtype=F32)))
    idx = jnp.arange(C, dtype=F32)
    diff = idx[:, None] - idx[None, :]
    decay = jnp.where(diff >= 0, jnp.exp(jnp.maximum(diff, 0.0)[None] * log_g[:, None, None]), 0.0)
    q_dec = jnp.exp((idx + 1.0)[None] * log_g[:, None])
    k_dec = jnp.exp((C - 1.0 - idx)[None] * log_g[:, None])
    c_dec = jnp.exp(C * log_g)
    return (jnp.cos(ang), jnp.sin(ang), decay,
            jnp.broadcast_to(q_dec[:, :, None], (H, C, dv)),
            jnp.broadcast_to(k_dec[:, :, None], (H, C, dk)),
            jnp.broadcast_to(c_dec[:, None, None], (H, 1, dv)))


def _oproj_ln_kernel(a_ref, w_ref, x_ref, g_ref, lg_ref, lb_ref, o_ref):
    y = jnp.dot(a_ref[0], w_ref[...], preferred_element_type=F32)
    z = ALPHA * x_ref[0] + g_ref[0] * y
    o_ref[0] = _layer_norm(z, lg_ref[...], lb_ref[...])


def _oproj_ln(a, w, x, gate, lg, lb, tm=512):
    B, S, D = x.shape
    K = a.shape[-1]
    vec = pl.BlockSpec((1, 1, D), lambda b, s: (b, 0, 0))
    par = pl.BlockSpec((1, D), lambda b, s: (0, 0))
    return pl.pallas_call(
        _oproj_ln_kernel,
        grid=(B, S // tm),
        in_specs=[pl.BlockSpec((1, tm, K), lambda b, s: (b, s, 0)),
                  pl.BlockSpec((K, D), lambda b, s: (0, 0)),
                  pl.BlockSpec((1, tm, D), lambda b, s: (b, s, 0)),
                  vec, par, par],
        out_specs=pl.BlockSpec((1, tm, D), lambda b, s: (b, s, 0)),
        out_shape=jax.ShapeDtypeStruct((B, S, D), F32),
        compiler_params=_params("parallel", "parallel"),
        name="oproj_ln",
    )(a, w, x, gate, lg.reshape(1, D), lb.reshape(1, D))


def _top_values(s, k):
    vals = []
    for _ in range(k):
        m = jnp.max(s, axis=0, keepdims=True)
        vals.append(m)
        s = jnp.where(s == m, -jnp.inf, s)
    return vals


def _top_values_ranked(s, k):
    vals = []
    rank = jnp.full(s.shape, float(k), F32)
    for i in range(k):
        m = jnp.max(s, axis=0, keepdims=True)
        vals.append(m)
        hit = s == m
        rank = jnp.where(hit, float(i), rank)
        s = jnp.where(hit, -jnp.inf, s)
    return vals, rank


def _twice_bf16(x):
    b = pltpu.bitcast(x, jnp.uint32)
    return b | (b >> 16)


_PEER_PAIRS = [(p, q) for p in range(PEER_TOPK) for q in range(PEER_TOPK) if (p + 1) * (q + 1) <= PEER_TOPK + 1]
_PEER_CAND_ROWS = -(-len(_PEER_PAIRS) // 8) * 8


def _peer_kernel(x_ref, sc_ref, sh_ref, g_ref, wq_ref, keys_ref, u_ref, vt_ref, lg_ref, lb_ref, o_ref,
                 h_ref, w1w_ref, lw_ref, w2b_ref, r2b_ref, cand_ref, a_ref, ws_ref, p_ref, acc_ref, *, te, sub, kchunk, wb):
    e = pl.program_id(2)
    tm = h_ref.shape[0]

    @pl.when(e == 0)
    def _route():
        h = (x_ref[0] * (1.0 + sc_ref[0]) + sh_ref[0]).astype(BF16)
        h_ref[...] = h
        q_t = lax.dot_general(wq_ref[...], h, _NT, preferred_element_type=F32).astype(BF16)
        dq = keys_ref.shape[2]
        cand_ref[...] = jnp.full(cand_ref.shape, -jnp.inf, F32)
        for hd in range(PEER_HEADS):
            s1_all = jnp.dot(keys_ref[2 * hd], q_t[(2 * hd) * dq:(2 * hd + 1) * dq], preferred_element_type=F32)
            s2_all = jnp.dot(keys_ref[2 * hd + 1], q_t[(2 * hd + 1) * dq:(2 * hd + 2) * dq], preferred_element_type=F32)
            for lt in range(tm // LANES):
                ln = slice(lt * LANES, (lt + 1) * LANES)
                s1, s2 = s1_all[:, ln], s2_all[:, ln]
                a1 = _top_values(s1, PEER_TOPK)
                a2, rank2 = _top_values_ranked(s2, PEER_TOPK)
                for i, (p, q) in enumerate(_PEER_PAIRS):
                    cand_ref[i:i + 1, ln] = a1[p] + a2[q]
                c = _top_values(cand_ref[:, ln], PEER_TOPK + 1)
                z = jnp.ones_like(c[0])
                for kk in range(1, PEER_TOPK):
                    z = z + jnp.exp(c[kk] - c[0])
                inv_z = 1.0 / z
                tau = 0.5 * (c[PEER_TOPK - 1] + c[PEER_TOPK])
                n_ok = jnp.zeros_like(s1)
                for q in range(PEER_TOPK):
                    n_ok = jnp.where(s1 >= tau - a2[q], float(q + 1), n_ok)
                w1 = (jnp.exp(s1 - a1[0]) * inv_z).astype(BF16).astype(F32)
                by_step = lambda t: t.reshape(w1w_ref.shape[1], w1w_ref.shape[2], LANES)
                w1w_ref[hd, :, :, ln] = by_step(_twice_bf16(w1))
                lw_ref[hd, :, :, ln] = by_step(_twice_bf16(n_ok))
                w2b_ref[hd, :, ln] = jnp.exp(s2 - a2[0]).astype(BF16)
                r2b_ref[hd, :, ln] = rank2.astype(BF16)
        acc_ref[...] = jnp.zeros_like(acc_ref)

    per_sub = sub // PEER_NKEYS
    n_sub = te // sub

    def activations(sb):
        a_ref[sb] = lax.dot_general(u_ref[sb * sub:(sb + 1) * sub, :], h_ref[...], _NT,
                                    preferred_element_type=F32)

    activations(0)
    for sb in range(n_sub):
        for ii in range(per_sub):
            k = sb * per_sub + ii
            for lt in range(tm // wb):
                ln = slice(lt * wb, (lt + 1) * wb)
                wsum = None
                for hd in range(PEER_HEADS):
                    row_bf16 = lambda ref: pltpu.bitcast(
                        jnp.broadcast_to(ref[hd, e, k:k + 1, ln], (PEER_NKEYS // 2, wb)), BF16)
                    keep = r2b_ref[hd, :, ln] < row_bf16(lw_ref)
                    term = jnp.where(keep, w2b_ref[hd, :, ln], jnp.zeros((), BF16)) * row_bf16(w1w_ref)
                    wsum = term if wsum is None else wsum + term
                ws_ref[ii * PEER_NKEYS:(ii + 1) * PEER_NKEYS, ln] = wsum
        if sb + 1 < n_sub:
            activations(sb + 1)
        for ii in range(per_sub):
            rows = slice(ii * PEER_NKEYS, (ii + 1) * PEER_NKEYS)
            p_ref[pl.ds(sb * sub + ii * PEER_NKEYS, PEER_NKEYS), :] = (
                _gelu_tanh(a_ref[sb, rows, :]).astype(BF16) * ws_ref[rows, :])
        done = (sb + 1) * sub
        if done % kchunk == 0:
            cols = slice(done - kchunk, done)
            acc_ref[...] += jnp.dot(vt_ref[:, cols], p_ref[cols, :], preferred_element_type=F32)

    @pl.when(e == pl.num_programs(2) - 1)
    def _finish():
        y = acc_ref[...].T
        z = ALPHA * x_ref[0] + g_ref[0] * y
        o_ref[0] = _layer_norm(z, lg_ref[...], lb_ref[...])


def _peer(x, sc, sh, gate, wq_t, keys, u, v_t, layer, lg, lb, tm=512, te=2048, sub=256, kchunk=1024, wb=256):
    B, S, D = x.shape
    E = u.shape[1]
    nq = wq_t.shape[0]
    vec = pl.BlockSpec((1, 1, D), lambda b, s, e: (b, 0, 0))
    par = pl.BlockSpec((1, D), lambda b, s, e: (0, 0))
    return pl.pallas_call(
        functools.partial(_peer_kernel, te=te, sub=sub, kchunk=kchunk, wb=wb),
        grid=(B, S // tm, E // te),
        in_specs=[pl.BlockSpec((1, tm, D), lambda b, s, e: (b, s, 0)), vec, vec, vec,
                  pl.BlockSpec((nq, D), lambda b, s, e: (0, 0), pipeline_mode=pl.Buffered(1)),
                  pl.BlockSpec(keys.shape, lambda b, s, e: (0, 0, 0), pipeline_mode=pl.Buffered(1)),
                  pl.BlockSpec((None, te, D), lambda b, s, e: (layer, e, 0)),
                  pl.BlockSpec((None, D, te), lambda b, s, e: (layer, 0, e)),
                  par, par],
        out_specs=pl.BlockSpec((1, tm, D), lambda b, s, e: (b, s, 0)),
        out_shape=jax.ShapeDtypeStruct((B, S, D), F32),
        scratch_shapes=[pltpu.VMEM((tm, D), BF16),
                        pltpu.VMEM((PEER_HEADS, E // te, te // PEER_NKEYS, tm), jnp.uint32),
                        pltpu.VMEM((PEER_HEADS, E // te, te // PEER_NKEYS, tm), jnp.uint32),
                        pltpu.VMEM((PEER_HEADS, PEER_NKEYS, tm), BF16),
                        pltpu.VMEM((PEER_HEADS, PEER_NKEYS, tm), BF16),
                        pltpu.VMEM((_PEER_CAND_ROWS, tm), F32),
                        pltpu.VMEM((te // sub, sub, tm), F32),
                        pltpu.VMEM((sub, tm), BF16),
                        pltpu.VMEM((te, tm), BF16),
                        pltpu.VMEM((D, tm), F32)],
        compiler_params=_params("parallel", "parallel", "arbitrary"),
        name="peer",
    )(x, sc, sh, gate, wq_t, keys, u, v_t, lg.reshape(1, D), lb.reshape(1, D))


def _compress_kernel(pc_ref, pe_ref, w1_ref, b1_ref, w2_ref, w2t_ref, o_ref, ot_ref):
    pc = pc_ref[0, 0, 0]
    half = pc.shape[1]
    lo = (pc + pe_ref[0, 0:1, :]).astype(BF16)
    hi = (pc + pe_ref[0, 1:2, :]).astype(BF16)
    a = jnp.dot(lo, w1_ref[0, :half, :], preferred_element_type=F32)
    b = jnp.dot(hi, w1_ref[0, half:, :], preferred_element_type=F32)
    b_next = pltpu.roll(b, pc.shape[0] - 1, 0)
    hid = jax.nn.gelu(a + b_next + b1_ref[0]).astype(BF16)
    o_ref[0, 0, 0] = jnp.dot(hid, w2_ref[0], preferred_element_type=F32).astype(o_ref.dtype)
    ot_ref[0, 0, 0] = lax.dot_general(w2t_ref[0], hid, _NT, preferred_element_type=F32).astype(ot_ref.dtype)


def _compress(pieces, pe2, w1, b1, w2, w2t):
    _, B, G, NP, F = pieces.shape
    Hd = w1.shape[2]
    hd = w2.shape[2]
    return pl.pallas_call(
        _compress_kernel,
        grid=(2, B, G),
        in_specs=[pl.BlockSpec((1, 1, 1, NP, F), lambda c, b, g: (c, b, g, 0, 0)),
                  pl.BlockSpec((1, 2, F), lambda c, b, g: (c, 0, 0)),
                  pl.BlockSpec((1, 2 * F, Hd), lambda c, b, g: (c, 0, 0)),
                  pl.BlockSpec((1, 1, Hd), lambda c, b, g: (c, 0, 0)),
                  pl.BlockSpec((1, Hd, hd), lambda c, b, g: (c, 0, 0)),
                  pl.BlockSpec((1, hd, Hd), lambda c, b, g: (c, 0, 0))],
        out_specs=[pl.BlockSpec((1, 1, 1, NP, hd), lambda c, b, g: (c, b, g, 0, 0)),
                   pl.BlockSpec((1, 1, 1, hd, NP), lambda c, b, g: (c, b, g, 0, 0))],
        out_shape=[jax.ShapeDtypeStruct((2, B, G, NP, hd), BF16),
                   jax.ShapeDtypeStruct((2, B, G, hd, NP), BF16)],
        compiler_params=_params("parallel", "parallel", "parallel"),
        name="compress",
    )(pieces, pe2, w1, b1, w2, w2t)


def _softmax_t(s, mask):
    sm = jnp.where(mask, s, MASKED)
    m = jnp.maximum(jnp.max(sm, axis=0, keepdims=True), NEG_INF)
    e = jnp.exp2(sm - m)
    return e, 1.0 / jnp.maximum(jnp.sum(e, axis=0, keepdims=True), 1e-30)


def _nsa_kernel(qt_ref, gl_ref, kc_ref, vct_ref, ks_ref, vst_ref, kw_ref, vwt_ref, c2s_ref, o_ref,
                sel_ref, sc_ref, sw_ref, sa_ref, sb_ref, pa_ref, pb_ref, acc_ref):
    R, hd, QB = NSA_REP, NSA_HEAD_DIM, Q_BLOCK
    g = pl.program_id(1)
    start = pl.program_id(2) * QB
    q4 = qt_ref[0] * LOG2E
    q_t = jnp.concatenate([q4[r * hd:(r + 1) * hd] for r in range(R)], axis=1).astype(BF16)
    t_q = start + lax.broadcasted_iota(jnp.int32, (1, QB), 1)
    lanes = lambda r: slice(r * QB, (r + 1) * QB)
    dot = lambda a, b: jnp.dot(a, b, preferred_element_type=F32)
    n_tiles = ks_ref.shape[2] // SLC_TILE

    def k_slc_tile(kt):
        return ks_ref[0, 0, pl.ds(pl.multiple_of(kt * SLC_TILE, SLC_TILE), SLC_TILE), :]

    base = pl.multiple_of(jnp.maximum(start - WINDOW, 0), QB)
    sc_ref[...] = dot(kc_ref[0, 0], q_t)
    sw_ref[...] = dot(kw_ref[0, 0, pl.ds(base, WIN_KEYS), :], q_t)
    sa_ref[...] = dot(k_slc_tile(0), q_t)

    n_cp = kc_ref.shape[2]
    n_idx = lax.broadcasted_iota(jnp.int32, (n_cp, 1), 0)
    cmask = (n_idx * CMP_STRIDE + (CMP_LEN - 1)) <= t_q
    v_cmp_t = vct_ref[0, 0]
    o_cmp, psum = [], None
    for r in range(R):
        e, inv = _softmax_t(sc_ref[:, lanes(r)], cmask)
        p = e * inv
        o_cmp.append(dot(v_cmp_t, p.astype(BF16)))
        psum = p if psum is None else psum + p

    kp = base + lax.broadcasted_iota(jnp.int32, (WIN_KEYS, 1), 0)
    wmask = (kp <= t_q) & (kp > t_q - WINDOW)

    def window_head(r):
        e, inv = _softmax_t(sw_ref[:, lanes(r)], wmask)
        return dot(vwt_ref[0, :, pl.ds(base, WIN_KEYS)], e.astype(BF16)) * inv

    c2s = c2s_ref[...]
    imp = None
    for part in _split3(psum):
        d = dot(c2s, part)
        imp = d if imp is None else imp + d
    n_slc = c2s.shape[0]
    j_idx = lax.broadcasted_iota(jnp.int32, (n_slc, 1), 0)
    cur = jnp.right_shift(t_q, SLC_BLOCK.bit_length() - 1)
    forced = (j_idx == 0) | (j_idx == cur) | (j_idx == cur - 1)
    avail = (j_idx * SLC_BLOCK) <= t_q
    score = jnp.where(avail, imp + jnp.where(forced, FORCE_BONUS, 0.0), -1.0)
    work = score
    sel = jnp.zeros(score.shape, F32)
    n_sel = min(SLC_TOPK, n_slc)
    o_win = []
    for it in range(n_sel):
        mx = jnp.max(work, axis=0, keepdims=True)
        first = jnp.min(jnp.where(work == mx, j_idx, n_slc), axis=0, keepdims=True)
        hit = j_idx == first
        sel = jnp.where(hit, 1.0, sel)
        work = jnp.where(hit, -jnp.inf, work)
        if (it + 1) % max(n_sel // R, 1) == 0 and len(o_win) < R:
            o_win.append(window_head(len(o_win)))
    o_win += [window_head(r) for r in range(len(o_win), R)]
    sel_ref[...] = jnp.where(score >= 0.0, sel, 0.0)

    per_tile = SLC_TILE // SLC_BLOCK
    row_tok = lax.broadcasted_iota(jnp.int32, (SLC_TILE, 1), 0)
    last = start // SLC_TILE

    def scores(kt):
        return dot(k_slc_tile(jnp.minimum(kt, n_tiles - 1)), q_t)

    def softmax_update(s_buf, p_buf, kt, causal, m, l):
        blk = pl.multiple_of(jnp.minimum(kt, n_tiles - 1) * per_tile, per_tile)
        sel_rows = sel_ref[pl.ds(blk, per_tile), :]
        mask = jnp.concatenate(
            [jnp.broadcast_to(sel_rows[a:a + 1, :], (SLC_BLOCK, QB)) for a in range(per_tile)], axis=0) > 0.5
        if causal:
            mask = mask & ((kt * SLC_TILE + row_tok) <= t_q)
        m_out, l_out, alphas = [], [], []
        for r in range(R):
            sm = jnp.where(mask, s_buf[:, lanes(r)], MASKED)
            m_new = jnp.maximum(m[r], jnp.max(sm, axis=0, keepdims=True))
            p = jnp.exp2(sm - m_new)
            alphas.append(jnp.exp2(m[r] - m_new))
            p_buf[:, lanes(r)] = p.astype(BF16)
            m_out.append(m_new)
            l_out.append(l[r] * alphas[r] + jnp.sum(p, axis=0, keepdims=True))
        return tuple(m_out), tuple(l_out), jnp.concatenate(alphas, axis=1)

    def v_tile_t(kt):
        k0 = pl.multiple_of(jnp.clip(kt, 0, n_tiles - 1) * SLC_TILE, SLC_TILE)
        return vst_ref[0, :, pl.ds(k0, SLC_TILE)]

    def pair_step(t0, carry, causal):
        m, l, alpha_prev = carry
        m, l, alpha0 = softmax_update(sa_ref, pa_ref, t0, causal, m, l)
        sb_ref[...] = scores(t0 + 1)
        acc_ref[...] = acc_ref[...] * alpha_prev + dot(v_tile_t(t0 - 1), pb_ref[...])
        m, l, alpha1 = softmax_update(sb_ref, pb_ref, t0 + 1, causal, m, l)
        sa_ref[...] = scores(t0 + 2)
        acc_ref[...] = acc_ref[...] * alpha0 + dot(v_tile_t(t0), pa_ref[...])
        return m, l, alpha1

    pb_ref[...] = jnp.zeros_like(pb_ref)
    acc_ref[...] = jnp.zeros_like(acc_ref)
    init = (tuple(jnp.full((1, QB), NEG_INF, F32) for _ in range(R)),
            tuple(jnp.zeros((1, QB), F32) for _ in range(R)),
            jnp.ones((1, R * QB), F32))
    n_pairs = last // 2
    carry = lax.fori_loop(0, n_pairs, lambda i, c: pair_step(2 * i, c, False), init)
    _, l, alpha = pair_step(2 * n_pairs, carry, True)
    acc = acc_ref[...] * alpha + dot(v_tile_t(2 * n_pairs + 1), pb_ref[...])
    o_slc = [acc[:, lanes(r)] * (1.0 / jnp.maximum(l[r], 1e-30)) for r in range(R)]

    outs = []
    for r in range(R):
        row = (g * R + r) * N_BRANCH
        gate = [jax.nn.sigmoid(gl_ref[0, pl.ds(row + br, 1), :]) for br in range(N_BRANCH)]
        outs.append(gate[0] * o_cmp[r] + gate[1] * o_slc[r] + gate[2] * o_win[r])
    o_ref[0] = jnp.concatenate(outs, axis=0).T.astype(o_ref.dtype)


def _nsa_attention(proj_t, shared, c2s_t):
    k_cmp, v_cmp_t, k_slc, k_win, v_t = shared
    B, _, S = proj_t.shape
    G, R, hd, QB = NSA_GROUPS, NSA_REP, NSA_HEAD_DIM, Q_BLOCK
    n_cp = k_cmp.shape[2]
    n_slc = S // SLC_BLOCK
    gate_blk = (NSA_HEADS * hd) // LANES
    whole = lambda shape: pl.BlockSpec((1, 1) + shape, lambda b, g, q: (b, g) + (0,) * len(shape))
    v_rows = lambda branch: pl.BlockSpec((1, hd, S), lambda b, g, q: (b, branch * G + g, 0))
    return pl.pallas_call(
        _nsa_kernel,
        grid=(B, G, S // QB),
        in_specs=[pl.BlockSpec((1, R * hd, QB), lambda b, g, q: (b, g, q)),
                  pl.BlockSpec((1, LANES, QB), lambda b, g, q: (b, gate_blk, q)),
                  whole((n_cp, hd)), whole((hd, n_cp)),
                  whole((S, hd)), v_rows(0),
                  whole((S, hd)), v_rows(1),
                  pl.BlockSpec((n_slc, n_cp), lambda b, g, q: (0, 0))],
        out_specs=pl.BlockSpec((1, QB, R * hd), lambda b, g, q: (b, q, g)),
        out_shape=jax.ShapeDtypeStruct((B, S, NSA_HEADS * hd), BF16),
        scratch_shapes=[pltpu.VMEM((n_slc, QB), F32),
                        pltpu.VMEM((n_cp, R * QB), F32),
                        pltpu.VMEM((WIN_KEYS, R * QB), F32),
                        pltpu.VMEM((SLC_TILE, R * QB), F32),
                        pltpu.VMEM((SLC_TILE, R * QB), F32),
                        pltpu.VMEM((SLC_TILE, R * QB), BF16),
                        pltpu.VMEM((SLC_TILE, R * QB), BF16),
                        pltpu.VMEM((hd, R * QB), F32)],
        compiler_params=_params("parallel", "parallel", "arbitrary"),
        name="nsa_attention",
    )(proj_t, proj_t, k_cmp, v_cmp_t, k_slc, v_t, k_win, v_t, c2s_t)


def _cmp_to_slc_t(n_cp, n_slc):
    i = np.arange(n_cp)[None, :] * CMP_STRIDE
    j = np.arange(n_slc)[:, None] * SLC_BLOCK
    ov = np.clip(np.minimum(i + CMP_LEN, j + SLC_BLOCK) - np.maximum(i, j), 0, None) / CMP_LEN
    ov[:, n_cp - 1] = 0.0
    return jnp.asarray(ov, dtype=BF16)


def _nsa_shared_kv(x, kv_sc, kv_sh, w_kv, cmp_pe, cmp_w1, cmp_b1, cmp_w2):
    B, S, _ = x.shape
    G, hd = NSA_GROUPS, NSA_HEAD_DIM
    kv = _modmm(x, kv_sc, kv_sh, w_kv.astype(BF16), transposed=False, tn=512)
    kv = kv.reshape(B, S, N_BRANCH, 2, G, hd)
    NP = S // CMP_STRIDE
    pieces = kv[:, :, 0].reshape(B, NP, CMP_STRIDE, 2, G, hd).transpose(3, 0, 4, 1, 2, 5)
    pieces = pieces.reshape(2, B, G, NP, CMP_STRIDE * hd)
    comp, comp_t = _compress(pieces, cmp_pe.reshape(2, 2, CMP_STRIDE * hd), cmp_w1.astype(BF16),
                             cmp_b1[:, None, :], cmp_w2.astype(BF16), cmp_w2.transpose(0, 2, 1).astype(BF16))
    to_k = lambda t: t.transpose(0, 2, 1, 3).astype(BF16)
    w_v = w_kv.reshape(-1, N_BRANCH, 2, G * hd)[:, 1:, 1].reshape(-1, 2 * G * hd)
    v_t = _modmm(x, kv_sc, kv_sh, w_v.T.astype(BF16), transposed=True, tn=2 * G * hd, out_dtype=BF16)
    return comp[0], comp_t[1], to_k(kv[:, :, 1, 0]), to_k(kv[:, :, 2, 0]), v_t


def kernel(x, c, ada_w, ada_b, ln_g, ln_b, ret_w_in, ret_w_o, kv_ada_w, kv_ada_b, nsa_w_kv, cmp_pe, cmp_w1, cmp_b1, cmp_w2, nsa_w_in, nsa_w_o, peer_w_q, peer_keys, peer_u, peer_v):
    B, S, D = x.shape
    mods = _cmod(c, ada_w, ada_b)
    kv_mod = _cmod(c, kv_ada_w[None], kv_ada_b[None])[0]
    ret_tables = _retention_tables(S, D // RET_HEADS)
    hd = NSA_HEAD_DIM
    n_gate = NSA_HEADS * N_BRANCH
    c2s_t = _cmp_to_slc_t(S // CMP_STRIDE, S // SLC_BLOCK)
    u_all = peer_u.astype(BF16)
    v_t_all = peer_v.transpose(0, 2, 1).astype(BF16)
    shared = None
    for layer in range(DEPTH):
        sh1, sc1, g1, sh2, sc2, g2 = [m[:, None, :] for m in jnp.split(mods[layer], 6, axis=-1)]
        if layer < N_A_LAYERS:
            proj = _modmm(x, sc1, sh1, ret_w_in[layer].astype(BF16), transposed=False, tn=1024, tm=1024)
            a = _retention(proj, ret_tables)
            w_o = ret_w_o[layer]
        else:
            lb = layer - N_A_LAYERS
            w_in = nsa_w_in[lb]
            w_t = jnp.concatenate([w_in[:, :NSA_HEADS * hd].T * (hd ** -0.5), w_in[:, NSA_HEADS * hd:].T,
                                   jnp.zeros((LANES - n_gate, D), F32)], axis=0).astype(BF16)
            proj_t = _modmm(x, sc1, sh1, w_t, transposed=True, tn=384)
            a = _nsa_attention(proj_t, shared, c2s_t)
            w_o = nsa_w_o[lb]
        x = _oproj_ln(a, w_o.astype(BF16), x, g1, ln_g[layer, 0], ln_b[layer, 0])
        keys = peer_keys[layer].reshape(PEER_HEADS * 2, PEER_NKEYS, -1).astype(BF16)
        x = _peer(x, sc2, sh2, g2, peer_w_q[layer].T.astype(BF16), keys, u_all, v_t_all, layer,
                  ln_g[layer, 1], ln_b[layer, 1])
        if layer == N_A_LAYERS - 1:
            kv_sh, kv_sc = [m[:, None, :] for m in jnp.split(kv_mod, 2, axis=-1)]
            shared = _nsa_shared_kv(x, kv_sc, kv_sh, nsa_w_kv, cmp_pe, cmp_w1, cmp_b1, cmp_w2)
    return x
```

```python
import functools

import numpy as np
import jax
import jax.numpy as jnp
from jax import lax
from jax.experimental import pallas as pl
from jax.experimental.pallas import tpu as pltpu

DEPTH = 4
N_A_LAYERS = DEPTH // 2
ALPHA = (2.0 * DEPTH) ** 0.25
LN_EPS = 1e-5
NEG_INF = -1e30
MASKED = 2.0 * NEG_INF
LOG2E = 1.4426950408889634

RET_HEADS = 4
RET_CHUNK = 128

NSA_HEADS = 16
NSA_GROUPS = 4
NSA_REP = NSA_HEADS // NSA_GROUPS
NSA_HEAD_DIM = 64
N_BRANCH = 3
CMP_STRIDE = 16
CMP_LEN = 32
SLC_BLOCK = 64
SLC_TOPK = 16
WINDOW = 512
Q_BLOCK = 512
FORCE_BONUS = 100.0
SLC_TILE = 512
WIN_KEYS = WINDOW + Q_BLOCK

PEER_HEADS = 8
PEER_NKEYS = 128
PEER_TOPK = 16

LANES = 128
VMEM_LIMIT = 56 * 1024 * 1024

F32 = jnp.float32
BF16 = jnp.bfloat16
_NT = (((1,), (1,)), ((), ()))


def _params(*sem):
    return pltpu.CompilerParams(dimension_semantics=sem, vmem_limit_bytes=VMEM_LIMIT)


def _split3(a):
    hi = a.astype(BF16)
    r1 = a - hi.astype(F32)
    mid = r1.astype(BF16)
    lo = (r1 - mid.astype(F32)).astype(BF16)
    return hi, mid, lo


def _layer_norm(z, g, b):
    mu = jnp.mean(z, axis=-1, keepdims=True)
    zc = z - mu
    var = jnp.mean(zc * zc, axis=-1, keepdims=True)
    return zc * lax.rsqrt(var + LN_EPS) * g + b


_GELU_K1 = -2.0 * 0.7978845608028654 * LOG2E
_GELU_K3 = _GELU_K1 * 0.044715


def _gelu_tanh(x):
    return x * (1.0 / (1.0 + jnp.exp2(x * (_GELU_K3 * (x * x) + _GELU_K1))))


def _cmod_kernel(c_ref, w_ref, b_ref, o_ref):
    c = c_ref[...]
    ca = c * jax.nn.sigmoid(c)
    w = w_ref[0]
    c_hi, c_mid, c_lo = _split3(ca)
    w_hi, w_mid, w_lo = _split3(w)
    dot = lambda a, b: jnp.dot(a, b, preferred_element_type=F32)
    acc = dot(c_hi, w_hi) + (dot(c_hi, w_mid) + dot(c_mid, w_hi)) + (dot(c_hi, w_lo) + dot(c_mid, w_mid) + dot(c_lo, w_hi))
    o_ref[0] = acc + b_ref[0]


def _cmod(c, w, b):
    L, D, N = w.shape
    n_b = c.shape[0]
    B = 16
    c = jnp.zeros((B, D), F32).at[:n_b].set(c)
    tn = 512
    out = pl.pallas_call(
        _cmod_kernel,
        grid=(L, N // tn),
        in_specs=[pl.BlockSpec((B, D), lambda l, n: (0, 0)),
                  pl.BlockSpec((1, D, tn), lambda l, n: (l, 0, n)),
                  pl.BlockSpec((1, 1, tn), lambda l, n: (l, 0, n))],
        out_specs=pl.BlockSpec((1, B, tn), lambda l, n: (l, 0, n)),
        out_shape=jax.ShapeDtypeStruct((L, B, N), F32),
        compiler_params=_params("parallel", "parallel"),
        name="cmod",
    )(c, w, b.reshape(L, 1, N))
    return out[:, :n_b]


def _modmm_kernel(x_ref, sc_ref, sh_ref, w_ref, o_ref, h_ref, *, transposed):
    @pl.when(pl.program_id(2) == 0)
    def _():
        h = x_ref[0] * (1.0 + sc_ref[0]) + sh_ref[0]
        h_ref[...] = h.astype(BF16)

    if transposed:
        o = lax.dot_general(w_ref[...], h_ref[...], _NT, preferred_element_type=F32)
    else:
        o = jnp.dot(h_ref[...], w_ref[...], preferred_element_type=F32)
    o_ref[0] = o.astype(o_ref.dtype)


def _modmm(x, sc, sh, w, *, transposed, tn, out_dtype=F32, tm=512):
    B, S, D = x.shape
    N = w.shape[0] if transposed else w.shape[1]
    if transposed:
        w_spec = pl.BlockSpec((tn, D), lambda b, s, n: (n, 0))
        o_spec = pl.BlockSpec((1, tn, tm), lambda b, s, n: (b, n, s))
        o_shape = (B, N, S)
    else:
        w_spec = pl.BlockSpec((D, tn), lambda b, s, n: (0, n))
        o_spec = pl.BlockSpec((1, tm, tn), lambda b, s, n: (b, s, n))
        o_shape = (B, S, N)
    vec = pl.BlockSpec((1, 1, D), lambda b, s, n: (b, 0, 0))
    return pl.pallas_call(
        functools.partial(_modmm_kernel, transposed=transposed),
        grid=(B, S // tm, N // tn),
        in_specs=[pl.BlockSpec((1, tm, D), lambda b, s, n: (b, s, 0)), vec, vec, w_spec],
        out_specs=o_spec,
        out_shape=jax.ShapeDtypeStruct(o_shape, out_dtype),
        scratch_shapes=[pltpu.VMEM((tm, D), BF16)],
        compiler_params=_params("parallel", "parallel", "arbitrary"),
        name="modmm_t" if transposed else "modmm",
    )(x, sc, sh, w)


def _ret_kernel(q_ref, k_ref, v_ref, g_ref, cos_ref, sin_ref, decay_ref, qdec_ref, kdec_ref, cdec_ref,
                o_ref, state_ref, *, dk, cps):
    @pl.when(pl.program_id(2) == 0)
    def _():
        state_ref[...] = jnp.zeros_like(state_ref)

    C = RET_CHUNK
    half = dk // 2
    state = state_ref[...]
    for j in range(cps):
        rows = slice(j * C, (j + 1) * C)
        cos, sin = cos_ref[rows, :], sin_ref[rows, :]

        def rot(t):
            x1, x2 = t[:, :half], t[:, half:]
            return jnp.concatenate([x1 * cos - x2 * sin, x1 * sin + x2 * cos], axis=-1)

        q = rot(q_ref[0, rows, :])
        k = rot(k_ref[0, rows, :]) * (dk ** -0.5)
        vb = v_ref[0, rows, :].astype(BF16)
        qb = q.astype(BF16)
        s = lax.dot_general(qb, k.astype(BF16), _NT, preferred_element_type=F32) * decay_ref[0]
        inner = jnp.dot(s.astype(BF16), vb, preferred_element_type=F32)
        cross = jnp.dot(qb, state.astype(BF16), preferred_element_type=F32) * qdec_ref[0]
        kd_t = (k * kdec_ref[0]).T.astype(BF16)
        state = state * cdec_ref[0] + jnp.dot(kd_t, vb, preferred_element_type=F32)
        o = inner + cross
        mu = jnp.mean(o, axis=-1, keepdims=True)
        oc = o - mu
        var = jnp.mean(oc * oc, axis=-1, keepdims=True)
        on = oc * lax.rsqrt(var + LN_EPS)
        g = g_ref[0, rows, :]
        o_ref[0, rows, :] = (g * jax.nn.sigmoid(g) * on).astype(o_ref.dtype)
    state_ref[...] = state


def _retention(proj, tables, cps=8):
    B, S, n_in = proj.shape
    H, C = RET_HEADS, RET_CHUNK
    dk = n_in // (6 * H)
    dv = 2 * dk
    cos, sin, decay, qdec, kdec, cdec = tables
    T = cps * C
    return pl.pallas_call(
        functools.partial(_ret_kernel, dk=dk, cps=cps),
        grid=(B, H, S // T),
        in_specs=[
            pl.BlockSpec((1, T, dk), lambda b, h, c: (b, c, h)),
            pl.BlockSpec((1, T, dk), lambda b, h, c: (b, c, H + h)),
            pl.BlockSpec((1, T, dv), lambda b, h, c: (b, c, H + h)),
            pl.BlockSpec((1, T, dv), lambda b, h, c: (b, c, 2 * H + h)),
            pl.BlockSpec((T, dk // 2), lambda b, h, c: (c, 0)),
            pl.BlockSpec((T, dk // 2), lambda b, h, c: (c, 0)),
            pl.BlockSpec((1, C, C), lambda b, h, c: (h, 0, 0)),
            pl.BlockSpec((1, C, dv), lambda b, h, c: (h, 0, 0)),
            pl.BlockSpec((1, C, dk), lambda b, h, c: (h, 0, 0)),
            pl.BlockSpec((1, 1, dv), lambda b, h, c: (h, 0, 0)),
        ],
        out_specs=pl.BlockSpec((1, T, dv), lambda b, h, c: (b, c, h)),
        out_shape=jax.ShapeDtypeStruct((B, S, H * dv), BF16),
        scratch_shapes=[pltpu.VMEM((dk, dv), F32)],
        compiler_params=_params("parallel", "parallel", "arbitrary"),
        name="retention",
    )(proj, proj, proj, proj, cos, sin, decay, qdec, kdec, cdec)


def _retention_tables(S, dk):
    H, C = RET_HEADS, RET_CHUNK
    dv = 2 * dk
    pos = jnp.arange(S, dtype=F32)
    theta = 1.0 / (10000.0 ** jnp.linspace(0.0, 1.0, dk // 2, dtype=F32))
    ang = pos[:, None] * theta[None, :]
    log_g = jnp.log1p(-jnp.exp2(-5.0 - jnp.arange(H, dtype=F32)))
    idx = jnp.arange(C, dtype=F32)
    diff = idx[:, None] - idx[None, :]
    decay = jnp.where(diff >= 0, jnp.exp(jnp.maximum(diff, 0.0)[None] * log_g[:, None, None]), 0.0)
    q_dec = jnp.exp((idx + 1.0)[None] * log_g[:, None])
    k_dec = jnp.exp((C - 1.0 - idx)[None] * log_g[:, None])
    c_dec = jnp.exp(C * log_g)
    return (jnp.cos(ang), jnp.sin(ang), decay,
            jnp.broadcast_to(q_dec[:, :, None], (H, C, dv)),
            jnp.broadcast_to(k_dec[:, :, None], (H, C, dk)),
            jnp.broadcast_to(c_dec[:, None, None], (H, 1, dv)))


def _oproj_ln_kernel(a_ref, w_ref, x_ref, g_ref, lg_ref, lb_ref, o_ref):
    y = jnp.dot(a_ref[0], w_ref[...], preferred_element_type=F32)
    z = ALPHA * x_ref[0] + g_ref[0] * y
    o_ref[0] = _layer_norm(z, lg_ref[...], lb_ref[...])


def _oproj_ln(a, w, x, gate, lg, lb, tm=1024):
    B, S, D = x.shape
    K = a.shape[-1]
    vec = pl.BlockSpec((1, 1, D), lambda b, s: (b, 0, 0))
    par = pl.BlockSpec((1, D), lambda b, s: (0, 0))
    return pl.pallas_call(
        _oproj_ln_kernel,
        grid=(B, S // tm),
        in_specs=[pl.BlockSpec((1, tm, K), lambda b, s: (b, s, 0)),
                  pl.BlockSpec((K, D), lambda b, s: (0, 0)),
                  pl.BlockSpec((1, tm, D), lambda b, s: (b, s, 0)),
                  vec, par, par],
        out_specs=pl.BlockSpec((1, tm, D), lambda b, s: (b, s, 0)),
        out_shape=jax.ShapeDtypeStruct((B, S, D), F32),
        compiler_params=_params("parallel", "parallel"),
        name="oproj_ln",
    )(a, w, x, gate, lg.reshape(1, D), lb.reshape(1, D))


def _top_values(s, k):
    vals = []
    for _ in range(k):
        m = jnp.max(s, axis=0, keepdims=True)
        vals.append(m)
        s = jnp.where(s == m, -jnp.inf, s)
    return vals


_SUBLANES = 8


def _sorted_top16(s):
    n = 16
    v = [s[r * _SUBLANES:(r + 1) * _SUBLANES] for r in range(n)]

    def exchange(x, i, l, larger_first):
        hi, lo = jnp.maximum(x[i], x[l]), jnp.minimum(x[i], x[l])
        x[i], x[l] = (hi, lo) if larger_first else (lo, hi)

    k = 2
    while k <= n:
        j = k // 2
        while j >= 1:
            for i in range(n):
                if i ^ j > i:
                    exchange(v, i, i ^ j, (i & k) == 0)
            j //= 2
        k *= 2
    shift = _SUBLANES // 2
    while shift >= 1:
        other = [pltpu.roll(x, shift, 0) for x in v]
        v = [jnp.maximum(v[i], other[n - 1 - i]) for i in range(n)]
        j = n // 2
        while j >= 1:
            for i in range(n):
                if i ^ j > i:
                    exchange(v, i, i ^ j, True)
            j //= 2
        shift //= 2
    return [x[0:1] for x in v]


def _twice_bf16(x):
    b = pltpu.bitcast(x, jnp.uint32)
    return b | (b >> 16)


_PEER_PAIRS = [(p, q) for p in range(PEER_TOPK) for q in range(PEER_TOPK) if (p + 1) * (q + 1) <= PEER_TOPK + 1]
_PEER_CAND_ROWS = -(-len(_PEER_PAIRS) // 8) * 8


def _peer_kernel(x_ref, sc_ref, sh_ref, g_ref, wq_ref, keys_ref, u_ref, vt_ref, lg_ref, lb_ref, o_ref,
                 h_ref, w1w_ref, lw_ref, w2b_ref, r2b_ref, cand_ref, a_ref, ws_ref, p_ref, acc_ref, *, te, sub, kchunk, wb):
    e = pl.program_id(2)
    tm = h_ref.shape[0]

    @pl.when(e == 0)
    def _route():
        h = (x_ref[0] * (1.0 + sc_ref[0]) + sh_ref[0]).astype(BF16)
        h_ref[...] = h
        q_t = lax.dot_general(wq_ref[...], h, _NT, preferred_element_type=F32).astype(BF16)
        dq = keys_ref.shape[2]
        cand_ref[...] = jnp.full(cand_ref.shape, -jnp.inf, F32)
        for hd in range(PEER_HEADS):
            s1_all = jnp.dot(keys_ref[2 * hd], q_t[(2 * hd) * dq:(2 * hd + 1) * dq], preferred_element_type=F32)
            s2_all = jnp.dot(keys_ref[2 * hd + 1], q_t[(2 * hd + 1) * dq:(2 * hd + 2) * dq], preferred_element_type=F32)
            for lt in range(tm // LANES):
                ln = slice(lt * LANES, (lt + 1) * LANES)
                s1, s2 = s1_all[:, ln], s2_all[:, ln]
                a1 = _sorted_top16(s1)
                a2 = _sorted_top16(s2)
                rank2 = jnp.full(s2.shape, float(PEER_TOPK), F32)
                for q in reversed(range(PEER_TOPK)):
                    rank2 = jnp.where(s2 >= a2[q], float(q), rank2)
                for i, (p, q) in enumerate(_PEER_PAIRS):
                    cand_ref[i:i + 1, ln] = a1[p] + a2[q]
                c = _top_values(cand_ref[:, ln], PEER_TOPK + 1)
                z = jnp.ones_like(c[0])
                for kk in range(1, PEER_TOPK):
                    z = z + jnp.exp(c[kk] - c[0])
                inv_z = 1.0 / z
                tau = 0.5 * (c[PEER_TOPK - 1] + c[PEER_TOPK])
                n_ok = jnp.zeros_like(s1)
                for q in range(PEER_TOPK):
                    n_ok = jnp.where(s1 >= tau - a2[q], float(q + 1), n_ok)
                w1 = (jnp.exp(s1 - a1[0]) * inv_z).astype(BF16).astype(F32)
                by_step = lambda t: t.reshape(w1w_ref.shape[1], w1w_ref.shape[2], LANES)
                w1w_ref[hd, :, :, ln] = by_step(_twice_bf16(w1))
                lw_ref[hd, :, :, ln] = by_step(_twice_bf16(n_ok))
                w2b_ref[hd, :, ln] = jnp.exp(s2 - a2[0]).astype(BF16)
                r2b_ref[hd, :, ln] = rank2.astype(BF16)
        acc_ref[...] = jnp.zeros_like(acc_ref)

    per_sub = sub // PEER_NKEYS
    n_sub = te // sub

    def activations(sb):
        a_ref[sb] = lax.dot_general(u_ref[sb * sub:(sb + 1) * sub, :], h_ref[...], _NT,
                                    preferred_element_type=F32)

    activations(0)
    for sb in range(n_sub):
        for ii in range(per_sub):
            k = sb * per_sub + ii
            for lt in range(tm // wb):
                ln = slice(lt * wb, (lt + 1) * wb)
                wsum = None
                for hd in range(PEER_HEADS):
                    row_bf16 = lambda ref: pltpu.bitcast(
                        jnp.broadcast_to(ref[hd, e, k:k + 1, ln], (PEER_NKEYS // 2, wb)), BF16)
                    keep = r2b_ref[hd, :, ln] < row_bf16(lw_ref)
                    term = jnp.where(keep, w2b_ref[hd, :, ln], jnp.zeros((), BF16)) * row_bf16(w1w_ref)
                    wsum = term if wsum is None else wsum + term
                ws_ref[ii * PEER_NKEYS:(ii + 1) * PEER_NKEYS, ln] = wsum
        if sb + 1 < n_sub:
            activations(sb + 1)
        for ii in range(per_sub):
            rows = slice(ii * PEER_NKEYS, (ii + 1) * PEER_NKEYS)
            p_ref[pl.ds(sb * sub + ii * PEER_NKEYS, PEER_NKEYS), :] = (
                _gelu_tanh(a_ref[sb, rows, :]).astype(BF16) * ws_ref[rows, :])
        done = (sb + 1) * sub
        if done % kchunk == 0:
            cols = slice(done - kchunk, done)
            acc_ref[...] += jnp.dot(vt_ref[:, cols], p_ref[cols, :], preferred_element_type=F32)

    @pl.when(e == pl.num_programs(2) - 1)
    def _finish():
        y = acc_ref[...].T
        z = ALPHA * x_ref[0] + g_ref[0] * y
        o_ref[0] = _layer_norm(z, lg_ref[...], lb_ref[...])


def _peer(x, sc, sh, gate, wq_t, keys, u, v_t, layer, lg, lb, tm=512, te=2048, sub=256, kchunk=1024, wb=256):
    B, S, D = x.shape
    E = u.shape[1]
    nq = wq_t.shape[0]
    vec = pl.BlockSpec((1, 1, D), lambda b, s, e: (b, 0, 0))
    par = pl.BlockSpec((1, D), lambda b, s, e: (0, 0))
    return pl.pallas_call(
        functools.partial(_peer_kernel, te=te, sub=sub, kchunk=kchunk, wb=wb),
        grid=(B, S // tm, E // te),
        in_specs=[pl.BlockSpec((1, tm, D), lambda b, s, e: (b, s, 0)), vec, vec, vec,
                  pl.BlockSpec((nq, D), lambda b, s, e: (0, 0), pipeline_mode=pl.Buffered(1)),
                  pl.BlockSpec(keys.shape, lambda b, s, e: (0, 0, 0), pipeline_mode=pl.Buffered(1)),
                  pl.BlockSpec((None, te, D), lambda b, s, e: (layer, e, 0)),
                  pl.BlockSpec((None, D, te), lambda b, s, e: (layer, 0, e)),
                  par, par],
        out_specs=pl.BlockSpec((1, tm, D), lambda b, s, e: (b, s, 0)),
        out_shape=jax.ShapeDtypeStruct((B, S, D), F32),
        scratch_shapes=[pltpu.VMEM((tm, D), BF16),
                        pltpu.VMEM((PEER_HEADS, E // te, te // PEER_NKEYS, tm), jnp.uint32),
                        pltpu.VMEM((PEER_HEADS, E // te, te // PEER_NKEYS, tm), jnp.uint32),
                        pltpu.VMEM((PEER_HEADS, PEER_NKEYS, tm), BF16),
                        pltpu.VMEM((PEER_HEADS, PEER_NKEYS, tm), BF16),
                        pltpu.VMEM((_PEER_CAND_ROWS, tm), F32),
                        pltpu.VMEM((te // sub, sub, tm), F32),
                        pltpu.VMEM((sub, tm), BF16),
                        pltpu.VMEM((te, tm), BF16),
                        pltpu.VMEM((D, tm), F32)],
        compiler_params=_params("parallel", "parallel", "arbitrary"),
        name="peer",
    )(x, sc, sh, gate, wq_t, keys, u, v_t, lg.reshape(1, D), lb.reshape(1, D))


def _compress_kernel(pc_ref, pe_ref, w1_ref, b1_ref, w2_ref, w2t_ref, o_ref, ot_ref):
    pc = pc_ref[0, 0, 0]
    half = pc.shape[1]
    lo = (pc + pe_ref[0, 0:1, :]).astype(BF16)
    hi = (pc + pe_ref[0, 1:2, :]).astype(BF16)
    a = jnp.dot(lo, w1_ref[0, :half, :], preferred_element_type=F32)
    b = jnp.dot(hi, w1_ref[0, half:, :], preferred_element_type=F32)
    b_next = pltpu.roll(b, pc.shape[0] - 1, 0)
    hid = jax.nn.gelu(a + b_next + b1_ref[0]).astype(BF16)
    o_ref[0, 0, 0] = jnp.dot(hid, w2_ref[0], preferred_element_type=F32).astype(o_ref.dtype)
    ot_ref[0, 0, 0] = lax.dot_general(w2t_ref[0], hid, _NT, preferred_element_type=F32).astype(ot_ref.dtype)


def _compress(pieces, pe2, w1, b1, w2, w2t):
    _, B, G, NP, F = pieces.shape
    Hd = w1.shape[2]
    hd = w2.shape[2]
    return pl.pallas_call(
        _compress_kernel,
        grid=(2, B, G),
        in_specs=[pl.BlockSpec((1, 1, 1, NP, F), lambda c, b, g: (c, b, g, 0, 0)),
                  pl.BlockSpec((1, 2, F), lambda c, b, g: (c, 0, 0)),
                  pl.BlockSpec((1, 2 * F, Hd), lambda c, b, g: (c, 0, 0)),
                  pl.BlockSpec((1, 1, Hd), lambda c, b, g: (c, 0, 0)),
                  pl.BlockSpec((1, Hd, hd), lambda c, b, g: (c, 0, 0)),
                  pl.BlockSpec((1, hd, Hd), lambda c, b, g: (c, 0, 0))],
        out_specs=[pl.BlockSpec((1, 1, 1, NP, hd), lambda c, b, g: (c, b, g, 0, 0)),
                   pl.BlockSpec((1, 1, 1, hd, NP), lambda c, b, g: (c, b, g, 0, 0))],
        out_shape=[jax.ShapeDtypeStruct((2, B, G, NP, hd), BF16),
                   jax.ShapeDtypeStruct((2, B, G, hd, NP), BF16)],
        compiler_params=_params("parallel", "parallel", "parallel"),
        name="compress",
    )(pieces, pe2, w1, b1, w2, w2t)


def _softmax_t(s, mask):
    sm = jnp.where(mask, s, MASKED)
    m = jnp.maximum(jnp.max(sm, axis=0, keepdims=True), NEG_INF)
    e = jnp.exp2(sm - m)
    return e, 1.0 / jnp.maximum(jnp.sum(e, axis=0, keepdims=True), 1e-30)


def _nsa_kernel(qt_ref, gl_ref, kc_ref, vct_ref, ks_ref, vst_ref, kw_ref, vwt_ref, c2s_ref, o_ref,
                sel_ref, sc_ref, sw_ref, sa_ref, sb_ref, pa_ref, pb_ref, acc_ref):
    R, hd, QB = NSA_REP, NSA_HEAD_DIM, Q_BLOCK
    g = pl.program_id(1)
    start = pl.program_id(2) * QB
    q4 = qt_ref[0] * LOG2E
    q_t = jnp.concatenate([q4[r * hd:(r + 1) * hd] for r in range(R)], axis=1).astype(BF16)
    t_q = start + lax.broadcasted_iota(jnp.int32, (1, QB), 1)
    lanes = lambda r: slice(r * QB, (r + 1) * QB)
    dot = lambda a, b: jnp.dot(a, b, preferred_element_type=F32)
    n_tiles = ks_ref.shape[2] // SLC_TILE

    def k_slc_tile(kt):
        return ks_ref[0, 0, pl.ds(pl.multiple_of(kt * SLC_TILE, SLC_TILE), SLC_TILE), :]

    base = pl.multiple_of(jnp.maximum(start - WINDOW, 0), QB)
    sc_ref[...] = dot(kc_ref[0, 0], q_t)
    sw_ref[...] = dot(kw_ref[0, 0, pl.ds(base, WIN_KEYS), :], q_t)
    sa_ref[...] = dot(k_slc_tile(0), q_t)

    n_cp = kc_ref.shape[2]
    n_idx = lax.broadcasted_iota(jnp.int32, (n_cp, 1), 0)
    cmask = (n_idx * CMP_STRIDE + (CMP_LEN - 1)) <= t_q
    v_cmp_t = vct_ref[0, 0]
    o_cmp, psum = [], None
    for r in range(R):
        e, inv = _softmax_t(sc_ref[:, lanes(r)], cmask)
        p = e * inv
        o_cmp.append(dot(v_cmp_t, p.astype(BF16)))
        psum = p if psum is None else psum + p

    kp = base + lax.broadcasted_iota(jnp.int32, (WIN_KEYS, 1), 0)
    wmask = (kp <= t_q) & (kp > t_q - WINDOW)

    def window_head(r):
        e, inv = _softmax_t(sw_ref[:, lanes(r)], wmask)
        return dot(vwt_ref[0, :, pl.ds(base, WIN_KEYS)], e.astype(BF16)) * inv

    c2s = c2s_ref[...]
    imp = None
    for part in _split3(psum):
        d = dot(c2s, part)
        imp = d if imp is None else imp + d
    n_slc = c2s.shape[0]
    j_idx = lax.broadcasted_iota(jnp.int32, (n_slc, 1), 0)
    cur = jnp.right_shift(t_q, SLC_BLOCK.bit_length() - 1)
    forced = (j_idx == 0) | (j_idx == cur) | (j_idx == cur - 1)
    avail = (j_idx * SLC_BLOCK) <= t_q
    score = jnp.where(avail, imp + jnp.where(forced, FORCE_BONUS, 0.0), -1.0)
    work = score
    sel = jnp.zeros(score.shape, F32)
    n_sel = min(SLC_TOPK, n_slc)
    o_win = []
    for it in range(n_sel):
        mx = jnp.max(work, axis=0, keepdims=True)
        first = jnp.min(jnp.where(work == mx, j_idx, n_slc), axis=0, keepdims=True)
        hit = j_idx == first
        sel = jnp.where(hit, 1.0, sel)
        work = jnp.where(hit, -jnp.inf, work)
        if (it + 1) % max(n_sel // R, 1) == 0 and len(o_win) < R:
            o_win.append(window_head(len(o_win)))
    o_win += [window_head(r) for r in range(len(o_win), R)]
    sel_ref[...] = jnp.where(score >= 0.0, sel, 0.0)

    per_tile = SLC_TILE // SLC_BLOCK
    row_tok = lax.broadcasted_iota(jnp.int32, (SLC_TILE, 1), 0)
    last = start // SLC_TILE

    def scores(kt):
        return dot(k_slc_tile(jnp.minimum(kt, n_tiles - 1)), q_t)

    def softmax_update(s_buf, p_buf, kt, causal, m, l):
        blk = pl.multiple_of(jnp.minimum(kt, n_tiles - 1) * per_tile, per_tile)
        sel_rows = sel_ref[pl.ds(blk, per_tile), :]
        mask = jnp.concatenate(
            [jnp.broadcast_to(sel_rows[a:a + 1, :], (SLC_BLOCK, QB)) for a in range(per_tile)], axis=0) > 0.5
        if causal:
            mask = mask & ((kt * SLC_TILE + row_tok) <= t_q)
        m_out, l_out, alphas = [], [], []
        for r in range(R):
            sm = jnp.where(mask, s_buf[:, lanes(r)], MASKED)
            m_new = jnp.maximum(m[r], jnp.max(sm, axis=0, keepdims=True))
            p = jnp.exp2(sm - m_new)
            alphas.append(jnp.exp2(m[r] - m_new))
            p_buf[:, lanes(r)] = p.astype(BF16)
            m_out.append(m_new)
            l_out.append(l[r] * alphas[r] + jnp.sum(p, axis=0, keepdims=True))
        return tuple(m_out), tuple(l_out), jnp.concatenate(alphas, axis=1)

    def v_tile_t(kt):
        k0 = pl.multiple_of(jnp.clip(kt, 0, n_tiles - 1) * SLC_TILE, SLC_TILE)
        return vst_ref[0, :, pl.ds(k0, SLC_TILE)]

    def pair_step(t0, carry, causal):
        m, l, alpha_prev = carry
        m, l, alpha0 = softmax_update(sa_ref, pa_ref, t0, causal, m, l)
        sb_ref[...] = scores(t0 + 1)
        acc_ref[...] = acc_ref[...] * alpha_prev + dot(v_tile_t(t0 - 1), pb_ref[...])
        m, l, alpha1 = softmax_update(sb_ref, pb_ref, t0 + 1, causal, m, l)
        sa_ref[...] = scores(t0 + 2)
        acc_ref[...] = acc_ref[...] * alpha0 + dot(v_tile_t(t0), pa_ref[...])
        return m, l, alpha1

    pb_ref[...] = jnp.zeros_like(pb_ref)
    acc_ref[...] = jnp.zeros_like(acc_ref)
    init = (tuple(jnp.full((1, QB), NEG_INF, F32) for _ in range(R)),
            tuple(jnp.zeros((1, QB), F32) for _ in range(R)),
            jnp.ones((1, R * QB), F32))
    n_pairs = last // 2
    carry = lax.fori_loop(0, n_pairs, lambda i, c: pair_step(2 * i, c, False), init)
    _, l, alpha = pair_step(2 * n_pairs, carry, True)
    acc = acc_ref[...] * alpha + dot(v_tile_t(2 * n_pairs + 1), pb_ref[...])
    o_slc = [acc[:, lanes(r)] * (1.0 / jnp.maximum(l[r], 1e-30)) for r in range(R)]

    outs = []
    for r in range(R):
        row = (g * R + r) * N_BRANCH
        gate = [jax.nn.sigmoid(gl_ref[0, pl.ds(row + br, 1), :]) for br in range(N_BRANCH)]
        outs.append(gate[0] * o_cmp[r] + gate[1] * o_slc[r] + gate[2] * o_win[r])
    o_ref[0] = jnp.concatenate(outs, axis=0).T.astype(o_ref.dtype)


def _nsa_attention(proj_t, shared, c2s_t):
    k_cmp, v_cmp_t, k_slc, k_win, v_t = shared
    B, _, S = proj_t.shape
    G, R, hd, QB = NSA_GROUPS, NSA_REP, NSA_HEAD_DIM, Q_BLOCK
    n_cp = k_cmp.shape[2]
    n_slc = S // SLC_BLOCK
    gate_blk = (NSA_HEADS * hd) // LANES
    whole = lambda shape: pl.BlockSpec((1, 1) + shape, lambda b, g, q: (b, g) + (0,) * len(shape))
    v_rows = lambda branch: pl.BlockSpec((1, hd, S), lambda b, g, q: (b, branch * G + g, 0))
    return pl.pallas_call(
        _nsa_kernel,
        grid=(B, G, S // QB),
        in_specs=[pl.BlockSpec((1, R * hd, QB), lambda b, g, q: (b, g, q)),
                  pl.BlockSpec((1, LANES, QB), lambda b, g, q: (b, gate_blk, q)),
                  whole((n_cp, hd)), whole((hd, n_cp)),
                  whole((S, hd)), v_rows(0),
                  whole((S, hd)), v_rows(1),
                  pl.BlockSpec((n_slc, n_cp), lambda b, g, q: (0, 0))],
        out_specs=pl.BlockSpec((1, QB, R * hd), lambda b, g, q: (b, q, g)),
        out_shape=jax.ShapeDtypeStruct((B, S, NSA_HEADS * hd), BF16),
        scratch_shapes=[pltpu.VMEM((n_slc, QB), F32),
                        pltpu.VMEM((n_cp, R * QB), F32),
                        pltpu.VMEM((WIN_KEYS, R * QB), F32),
                        pltpu.VMEM((SLC_TILE, R * QB), F32),
                        pltpu.VMEM((SLC_TILE, R * QB), F32),
                        pltpu.VMEM((SLC_TILE, R * QB), BF16),
                        pltpu.VMEM((SLC_TILE, R * QB), BF16),
                        pltpu.VMEM((hd, R * QB), F32)],
        compiler_params=_params("parallel", "parallel", "arbitrary"),
        name="nsa_attention",
    )(proj_t, proj_t, k_cmp, v_cmp_t, k_slc, v_t, k_win, v_t, c2s_t)


def _cmp_to_slc_t(n_cp, n_slc):
    i = np.arange(n_cp)[None, :] * CMP_STRIDE
    j = np.arange(n_slc)[:, None] * SLC_BLOCK
    ov = np.clip(np.minimum(i + CMP_LEN, j + SLC_BLOCK) - np.maximum(i, j), 0, None) / CMP_LEN
    ov[:, n_cp - 1] = 0.0
    return jnp.asarray(ov, dtype=BF16)


def _nsa_shared_kv(x, kv_sc, kv_sh, w_kv, cmp_pe, cmp_w1, cmp_b1, cmp_w2):
    B, S, _ = x.shape
    G, hd = NSA_GROUPS, NSA_HEAD_DIM
    kv = _modmm(x, kv_sc, kv_sh, w_kv.astype(BF16), transposed=False, tn=512, tm=1024)
    kv = kv.reshape(B, S, N_BRANCH, 2, G, hd)
    NP = S // CMP_STRIDE
    pieces = kv[:, :, 0].reshape(B, NP, CMP_STRIDE, 2, G, hd).transpose(3, 0, 4, 1, 2, 5)
    pieces = pieces.reshape(2, B, G, NP, CMP_STRIDE * hd)
    comp, comp_t = _compress(pieces, cmp_pe.reshape(2, 2, CMP_STRIDE * hd), cmp_w1.astype(BF16),
                             cmp_b1[:, None, :], cmp_w2.astype(BF16), cmp_w2.transpose(0, 2, 1).astype(BF16))
    to_k = lambda t: t.transpose(0, 2, 1, 3).astype(BF16)
    w_v = w_kv.reshape(-1, N_BRANCH, 2, G * hd)[:, 1:, 1].reshape(-1, 2 * G * hd)
    v_t = _modmm(x, kv_sc, kv_sh, w_v.T.astype(BF16), transposed=True, tn=2 * G * hd, out_dtype=BF16)
    return comp[0], comp_t[1], to_k(kv[:, :, 1, 0]), to_k(kv[:, :, 2, 0]), v_t


def kernel(x, c, ada_w, ada_b, ln_g, ln_b, ret_w_in, ret_w_o, kv_ada_w, kv_ada_b, nsa_w_kv, cmp_pe, cmp_w1, cmp_b1, cmp_w2, nsa_w_in, nsa_w_o, peer_w_q, peer_keys, peer_u, peer_v):
    B, S, D = x.shape
    mods = _cmod(c, ada_w, ada_b)
    kv_mod = _cmod(c, kv_ada_w[None], kv_ada_b[None])[0]
    ret_tables = _retention_tables(S, D // RET_HEADS)
    hd = NSA_HEAD_DIM
    n_gate = NSA_HEADS * N_BRANCH
    c2s_t = _cmp_to_slc_t(S // CMP_STRIDE, S // SLC_BLOCK)
    u_all = peer_u.astype(BF16)
    v_t_all = peer_v.transpose(0, 2, 1).astype(BF16)
    shared = None
    for layer in range(DEPTH):
        sh1, sc1, g1, sh2, sc2, g2 = [m[:, None, :] for m in jnp.split(mods[layer], 6, axis=-1)]
        if layer < N_A_LAYERS:
            proj = _modmm(x, sc1, sh1, ret_w_in[layer].astype(BF16), transposed=False, tn=1024, tm=1024)
            a = _retention(proj, ret_tables)
            w_o = ret_w_o[layer]
        else:
            lb = layer - N_A_LAYERS
            w_in = nsa_w_in[lb]
            w_t = jnp.concatenate([w_in[:, :NSA_HEADS * hd].T * (hd ** -0.5), w_in[:, NSA_HEADS * hd:].T,
                                   jnp.zeros((LANES - n_gate, D), F32)], axis=0).astype(BF16)
            proj_t = _modmm(x, sc1, sh1, w_t, transposed=True, tn=384, tm=1024)
            a = _nsa_attention(proj_t, shared, c2s_t)
            w_o = nsa_w_o[lb]
        x = _oproj_ln(a, w_o.astype(BF16), x, g1, ln_g[layer, 0], ln_b[layer, 0])
        keys = peer_keys[layer].reshape(PEER_HEADS * 2, PEER_NKEYS, -1).astype(BF16)
        x = _peer(x, sc2, sh2, g2, peer_w_q[layer].T.astype(BF16), keys, u_all, v_t_all, layer,
                  ln_g[layer, 1], ln_b[layer, 1])
        if layer == N_A_LAYERS - 1:
            kv_sh, kv_sc = [m[:, None, :] for m in jnp.split(kv_mod, 2, axis=-1)]
            shared = _nsa_shared_kv(x, kv_sc, kv_sh, nsa_w_kv, cmp_pe, cmp_w1, cmp_b1, cmp_w2)
    return x
```

```python
import functools

import numpy as np
import jax
import jax.numpy as jnp
from jax import lax
from jax.experimental import pallas as pl
from jax.experimental.pallas import tpu as pltpu

DEPTH = 4
N_A_LAYERS = DEPTH // 2
ALPHA = (2.0 * DEPTH) ** 0.25
LN_EPS = 1e-5
NEG_INF = -1e30
MASKED = 2.0 * NEG_INF
LOG2E = 1.4426950408889634

RET_HEADS = 4
RET_CHUNK = 128

NSA_HEADS = 16
NSA_GROUPS = 4
NSA_REP = NSA_HEADS // NSA_GROUPS
NSA_HEAD_DIM = 64
N_BRANCH = 3
CMP_STRIDE = 16
CMP_LEN = 32
SLC_BLOCK = 64
SLC_TOPK = 16
WINDOW = 512
Q_BLOCK = 512
FORCE_BONUS = 100.0
SLC_TILE = 512
WIN_KEYS = WINDOW + Q_BLOCK

PEER_HEADS = 8
PEER_NKEYS = 128
PEER_TOPK = 16

LANES = 128
VMEM_LIMIT = 56 * 1024 * 1024

F32 = jnp.float32
BF16 = jnp.bfloat16
_NT = (((1,), (1,)), ((), ()))


def _params(*sem):
    return pltpu.CompilerParams(dimension_semantics=sem, vmem_limit_bytes=VMEM_LIMIT)


def _split3(a):
    hi = a.astype(BF16)
    r1 = a - hi.astype(F32)
    mid = r1.astype(BF16)
    lo = (r1 - mid.astype(F32)).astype(BF16)
    return hi, mid, lo


def _layer_norm(z, g, b):
    mu = jnp.mean(z, axis=-1, keepdims=True)
    zc = z - mu
    var = jnp.mean(zc * zc, axis=-1, keepdims=True)
    return zc * lax.rsqrt(var + LN_EPS) * g + b


_GELU_K1 = -2.0 * 0.7978845608028654 * LOG2E
_GELU_K3 = _GELU_K1 * 0.044715


def _gelu_tanh(x):
    return x * (1.0 / (1.0 + jnp.exp2(x * (_GELU_K3 * (x * x) + _GELU_K1))))


def _cmod_kernel(c_ref, w_ref, b_ref, o_ref):
    c = c_ref[...]
    ca = c * jax.nn.sigmoid(c)
    w = w_ref[0]
    c_hi, c_mid, c_lo = _split3(ca)
    w_hi, w_mid, w_lo = _split3(w)
    dot = lambda a, b: jnp.dot(a, b, preferred_element_type=F32)
    acc = dot(c_hi, w_hi) + (dot(c_hi, w_mid) + dot(c_mid, w_hi)) + (dot(c_hi, w_lo) + dot(c_mid, w_mid) + dot(c_lo, w_hi))
    o_ref[0] = acc + b_ref[0]


def _cmod(c, w, b):
    L, D, N = w.shape
    n_b = c.shape[0]
    B = 16
    c = jnp.zeros((B, D), F32).at[:n_b].set(c)
    tn = 512
    out = pl.pallas_call(
        _cmod_kernel,
        grid=(L, N // tn),
        in_specs=[pl.BlockSpec((B, D), lambda l, n: (0, 0)),
                  pl.BlockSpec((1, D, tn), lambda l, n: (l, 0, n)),
                  pl.BlockSpec((1, 1, tn), lambda l, n: (l, 0, n))],
        out_specs=pl.BlockSpec((1, B, tn), lambda l, n: (l, 0, n)),
        out_shape=jax.ShapeDtypeStruct((L, B, N), F32),
        compiler_params=_params("parallel", "parallel"),
        name="cmod",
    )(c, w, b.reshape(L, 1, N))
    return out[:, :n_b]


def _modmm_kernel(x_ref, sc_ref, sh_ref, w_ref, o_ref, h_ref, *, transposed):
    @pl.when(pl.program_id(2) == 0)
    def _():
        h = x_ref[0] * (1.0 + sc_ref[0]) + sh_ref[0]
        h_ref[...] = h.astype(BF16)

    if transposed:
        o = lax.dot_general(w_ref[...], h_ref[...], _NT, preferred_element_type=F32)
    else:
        o = jnp.dot(h_ref[...], w_ref[...], preferred_element_type=F32)
    o_ref[0] = o.astype(o_ref.dtype)


def _modmm(x, sc, sh, w, *, transposed, tn, out_dtype=F32, tm=512):
    B, S, D = x.shape
    N = w.shape[0] if transposed else w.shape[1]
    if transposed:
        w_spec = pl.BlockSpec((tn, D), lambda b, s, n: (n, 0))
        o_spec = pl.BlockSpec((1, tn, tm), lambda b, s, n: (b, n, s))
        o_shape = (B, N, S)
    else:
        w_spec = pl.BlockSpec((D, tn), lambda b, s, n: (0, n))
        o_spec = pl.BlockSpec((1, tm, tn), lambda b, s, n: (b, s, n))
        o_shape = (B, S, N)
    vec = pl.BlockSpec((1, 1, D), lambda b, s, n: (b, 0, 0))
    return pl.pallas_call(
        functools.partial(_modmm_kernel, transposed=transposed),
        grid=(B, S // tm, N // tn),
        in_specs=[pl.BlockSpec((1, tm, D), lambda b, s, n: (b, s, 0)), vec, vec, w_spec],
        out_specs=o_spec,
        out_shape=jax.ShapeDtypeStruct(o_shape, out_dtype),
        scratch_shapes=[pltpu.VMEM((tm, D), BF16)],
        compiler_params=_params("parallel", "parallel", "arbitrary"),
        name="modmm_t" if transposed else "modmm",
    )(x, sc, sh, w)


def _ret_kernel(q_ref, k_ref, v_ref, g_ref, cos_ref, sin_ref, decay_ref, qdec_ref, kdec_ref, cdec_ref,
                o_ref, state_ref, *, dk, cps):
    @pl.when(pl.program_id(2) == 0)
    def _():
        state_ref[...] = jnp.zeros_like(state_ref)

    C = RET_CHUNK
    half = dk // 2
    state = state_ref[...]
    for j in range(cps):
        rows = slice(j * C, (j + 1) * C)
        cos, sin = cos_ref[rows, :], sin_ref[rows, :]

        def rot(t):
            x1, x2 = t[:, :half], t[:, half:]
            return jnp.concatenate([x1 * cos - x2 * sin, x1 * sin + x2 * cos], axis=-1)

        q = rot(q_ref[0, rows, :])
        k = rot(k_ref[0, rows, :]) * (dk ** -0.5)
        vb = v_ref[0, rows, :].astype(BF16)
        qb = q.astype(BF16)
        s = lax.dot_general(qb, k.astype(BF16), _NT, preferred_element_type=F32) * decay_ref[0]
        inner = jnp.dot(s.astype(BF16), vb, preferred_element_type=F32)
        cross = jnp.dot(qb, state.astype(BF16), preferred_element_type=F32) * qdec_ref[0]
        kd_t = (k * kdec_ref[0]).T.astype(BF16)
        state = state * cdec_ref[0] + jnp.dot(kd_t, vb, preferred_element_type=F32)
        o = inner + cross
        mu = jnp.mean(o, axis=-1, keepdims=True)
        oc = o - mu
        var = jnp.mean(oc * oc, axis=-1, keepdims=True)
        on = oc * lax.rsqrt(var + LN_EPS)
        g = g_ref[0, rows, :]
        o_ref[0, rows, :] = (g * jax.nn.sigmoid(g) * on).astype(o_ref.dtype)
    state_ref[...] = state


def _retention(proj, tables, cps=8):
    B, S, n_in = proj.shape
    H, C = RET_HEADS, RET_CHUNK
    dk = n_in // (6 * H)
    dv = 2 * dk
    cos, sin, decay, qdec, kdec, cdec = tables
    T = cps * C
    return pl.pallas_call(
        functools.partial(_ret_kernel, dk=dk, cps=cps),
        grid=(B, H, S // T),
        in_specs=[
            pl.BlockSpec((1, T, dk), lambda b, h, c: (b, c, h)),
            pl.BlockSpec((1, T, dk), lambda b, h, c: (b, c, H + h)),
            pl.BlockSpec((1, T, dv), lambda b, h, c: (b, c, H + h)),
            pl.BlockSpec((1, T, dv), lambda b, h, c: (b, c, 2 * H + h)),
            pl.BlockSpec((T, dk // 2), lambda b, h, c: (c, 0)),
            pl.BlockSpec((T, dk // 2), lambda b, h, c: (c, 0)),
            pl.BlockSpec((1, C, C), lambda b, h, c: (h, 0, 0)),
            pl.BlockSpec((1, C, dv), lambda b, h, c: (h, 0, 0)),
            pl.BlockSpec((1, C, dk), lambda b, h, c: (h, 0, 0)),
            pl.BlockSpec((1, 1, dv), lambda b, h, c: (h, 0, 0)),
        ],
        out_specs=pl.BlockSpec((1, T, dv), lambda b, h, c: (b, c, h)),
        out_shape=jax.ShapeDtypeStruct((B, S, H * dv), BF16),
        scratch_shapes=[pltpu.VMEM((dk, dv), F32)],
        compiler_params=_params("parallel", "parallel", "arbitrary"),
        name="retention",
    )(proj, proj, proj, proj, cos, sin, decay, qdec, kdec, cdec)


def _retention_tables(S, dk):
    H, C = RET_HEADS, RET_CHUNK
    dv = 2 * dk
    pos = jnp.arange(S, dtype=F32)
    theta = 1.0 / (10000.0 ** jnp.linspace(0.0, 1.0, dk // 2, dtype=F32))
    ang = pos[:, None] * theta[None, :]
    log_g = jnp.log1p(-jnp.exp2(-5.0 - jnp.arange(H, dtype=F32)))
    idx = jnp.arange(C, dtype=F32)
    diff = idx[:, None] - idx[None, :]
    decay = jnp.where(diff >= 0, jnp.exp(jnp.maximum(diff, 0.0)[None] * log_g[:, None, None]), 0.0)
    q_dec = jnp.exp((idx + 1.0)[None] * log_g[:, None])
    k_dec = jnp.exp((C - 1.0 - idx)[None] * log_g[:, None])
    c_dec = jnp.exp(C * log_g)
    return (jnp.cos(ang), jnp.sin(ang), decay,
            jnp.broadcast_to(q_dec[:, :, None], (H, C, dv)),
            jnp.broadcast_to(k_dec[:, :, None], (H, C, dk)),
            jnp.broadcast_to(c_dec[:, None, None], (H, 1, dv)))


def _oproj_ln_kernel(a_ref, w_ref, x_ref, g_ref, lg_ref, lb_ref, o_ref):
    y = jnp.dot(a_ref[0], w_ref[...], preferred_element_type=F32)
    z = ALPHA * x_ref[0] + g_ref[0] * y
    o_ref[0] = _layer_norm(z, lg_ref[...], lb_ref[...])


def _oproj_ln(a, w, x, gate, lg, lb, tm=1024):
    B, S, D = x.shape
    K = a.shape[-1]
    vec = pl.BlockSpec((1, 1, D), lambda b, s: (b, 0, 0))
    par = pl.BlockSpec((1, D), lambda b, s: (0, 0))
    return pl.pallas_call(
        _oproj_ln_kernel,
        grid=(B, S // tm),
        in_specs=[pl.BlockSpec((1, tm, K), lambda b, s: (b, s, 0)),
                  pl.BlockSpec((K, D), lambda b, s: (0, 0)),
                  pl.BlockSpec((1, tm, D), lambda b, s: (b, s, 0)),
                  vec, par, par],
        out_specs=pl.BlockSpec((1, tm, D), lambda b, s: (b, s, 0)),
        out_shape=jax.ShapeDtypeStruct((B, S, D), F32),
        compiler_params=_params("parallel", "parallel"),
        name="oproj_ln",
    )(a, w, x, gate, lg.reshape(1, D), lb.reshape(1, D))


def _top_values(s, k):
    vals = []
    for _ in range(k):
        m = jnp.max(s, axis=0, keepdims=True)
        vals.append(m)
        s = jnp.where(s == m, -jnp.inf, s)
    return vals


_SUBLANES = 8


def _sorted_top16(s):
    n = 16
    v = [s[r * _SUBLANES:(r + 1) * _SUBLANES] for r in range(n)]

    def exchange(x, i, l, larger_first):
        hi, lo = jnp.maximum(x[i], x[l]), jnp.minimum(x[i], x[l])
        x[i], x[l] = (hi, lo) if larger_first else (lo, hi)

    k = 2
    while k <= n:
        j = k // 2
        while j >= 1:
            for i in range(n):
                if i ^ j > i:
                    exchange(v, i, i ^ j, (i & k) == 0)
            j //= 2
        k *= 2
    shift = _SUBLANES // 2
    while shift >= 1:
        other = [pltpu.roll(x, shift, 0) for x in v]
        v = [jnp.maximum(v[i], other[n - 1 - i]) for i in range(n)]
        j = n // 2
        while j >= 1:
            for i in range(n):
                if i ^ j > i:
                    exchange(v, i, i ^ j, True)
            j //= 2
        shift //= 2
    return [x[0:1] for x in v]


def _twice_bf16(x):
    b = pltpu.bitcast(x, jnp.uint32)
    return b | (b >> 16)


_PEER_PAIRS = [(p, q) for p in range(PEER_TOPK) for q in range(PEER_TOPK) if (p + 1) * (q + 1) <= PEER_TOPK + 1]
_PEER_CAND_ROWS = -(-len(_PEER_PAIRS) // 8) * 8


def _peer_kernel(x_ref, sc_ref, sh_ref, g_ref, wq_ref, keys_ref, u_ref, vt_ref, lg_ref, lb_ref, o_ref,
                 h_ref, w1w_ref, lw_ref, w2b_ref, r2b_ref, cand_ref, a_ref, ws_ref, p_ref, acc_ref, *, te, sub, kchunk, wb):
    e = pl.program_id(2)
    tm = h_ref.shape[0]

    @pl.when(e == 0)
    def _route():
        h = (x_ref[0] * (1.0 + sc_ref[0]) + sh_ref[0]).astype(BF16)
        h_ref[...] = h
        q_t = lax.dot_general(wq_ref[...], h, _NT, preferred_element_type=F32).astype(BF16)
        dq = keys_ref.shape[2]
        cand_ref[...] = jnp.full(cand_ref.shape, -jnp.inf, F32)
        for hd in range(PEER_HEADS):
            s1_all = jnp.dot(keys_ref[2 * hd], q_t[(2 * hd) * dq:(2 * hd + 1) * dq], preferred_element_type=F32)
            s2_all = jnp.dot(keys_ref[2 * hd + 1], q_t[(2 * hd + 1) * dq:(2 * hd + 2) * dq], preferred_element_type=F32)
            for lt in range(tm // LANES):
                ln = slice(lt * LANES, (lt + 1) * LANES)
                s1, s2 = s1_all[:, ln], s2_all[:, ln]
                a1 = _sorted_top16(s1)
                a2 = _sorted_top16(s2)
                rank2 = jnp.full(s2.shape, float(PEER_TOPK), F32)
                for q in reversed(range(PEER_TOPK)):
                    rank2 = jnp.where(s2 >= a2[q], float(q), rank2)
                for i, (p, q) in enumerate(_PEER_PAIRS):
                    cand_ref[i:i + 1, ln] = a1[p] + a2[q]
                c = _top_values(cand_ref[:, ln], PEER_TOPK + 1)
                z = jnp.ones_like(c[0])
                for kk in range(1, PEER_TOPK):
                    z = z + jnp.exp(c[kk] - c[0])
                inv_z = 1.0 / z
                tau = 0.5 * (c[PEER_TOPK - 1] + c[PEER_TOPK])
                n_ok = jnp.zeros_like(s1)
                for q in range(PEER_TOPK):
                    n_ok = jnp.where(s1 >= tau - a2[q], float(q + 1), n_ok)
                w1 = (jnp.exp(s1 - a1[0]) * inv_z).astype(BF16).astype(F32)
                by_step = lambda t: t.reshape(w1w_ref.shape[1], w1w_ref.shape[2], LANES)
                w1w_ref[hd, :, :, ln] = by_step(_twice_bf16(w1))
                lw_ref[hd, :, :, ln] = by_step(_twice_bf16(n_ok))
                w2b_ref[hd, :, ln] = jnp.exp(s2 - a2[0]).astype(BF16)
                r2b_ref[hd, :, ln] = rank2.astype(BF16)
        acc_ref[...] = jnp.zeros_like(acc_ref)

    per_sub = sub // PEER_NKEYS
    n_sub = te // sub

    def activations(sb):
        a_ref[sb] = lax.dot_general(u_ref[sb * sub:(sb + 1) * sub, :], h_ref[...], _NT,
                                    preferred_element_type=F32)

    activations(0)
    for sb in range(n_sub):
        for ii in range(per_sub):
            k = sb * per_sub + ii
            for lt in range(tm // wb):
                ln = slice(lt * wb, (lt + 1) * wb)
                wsum = None
                for hd in range(PEER_HEADS):
                    row_bf16 = lambda ref: pltpu.bitcast(
                        jnp.broadcast_to(ref[hd, e, k:k + 1, ln], (PEER_NKEYS // 2, wb)), BF16)
                    keep = r2b_ref[hd, :, ln] < row_bf16(lw_ref)
                    term = jnp.where(keep, w2b_ref[hd, :, ln], jnp.zeros((), BF16)) * row_bf16(w1w_ref)
                    wsum = term if wsum is None else wsum + term
                ws_ref[ii * PEER_NKEYS:(ii + 1) * PEER_NKEYS, ln] = wsum
        if sb + 1 < n_sub:
            activations(sb + 1)
        for ii in range(per_sub):
            rows = slice(ii * PEER_NKEYS, (ii + 1) * PEER_NKEYS)
            p_ref[pl.ds(sb * sub + ii * PEER_NKEYS, PEER_NKEYS), :] = (
                _gelu_tanh(a_ref[sb, rows, :]).astype(BF16) * ws_ref[rows, :])
        done = (sb + 1) * sub
        if done % kchunk == 0:
            cols = slice(done - kchunk, done)
            acc_ref[...] += jnp.dot(vt_ref[:, cols], p_ref[cols, :], preferred_element_type=F32)

    @pl.when(e == pl.num_programs(2) - 1)
    def _finish():
        y = acc_ref[...].T
        z = ALPHA * x_ref[0] + g_ref[0] * y
        o_ref[0] = _layer_norm(z, lg_ref[...], lb_ref[...])


def _peer(x, sc, sh, gate, wq_t, keys, u, v_t, layer, lg, lb, tm=512, te=2048, sub=256, kchunk=1024, wb=256):
    B, S, D = x.shape
    E = u.shape[1]
    nq = wq_t.shape[0]
    vec = pl.BlockSpec((1, 1, D), lambda b, s, e: (b, 0, 0))
    par = pl.BlockSpec((1, D), lambda b, s, e: (0, 0))
    return pl.pallas_call(
        functools.partial(_peer_kernel, te=te, sub=sub, kchunk=kchunk, wb=wb),
        grid=(B, S // tm, E // te),
        in_specs=[pl.BlockSpec((1, tm, D), lambda b, s, e: (b, s, 0)), vec, vec, vec,
                  pl.BlockSpec((nq, D), lambda b, s, e: (0, 0), pipeline_mode=pl.Buffered(1)),
                  pl.BlockSpec(keys.shape, lambda b, s, e: (0, 0, 0), pipeline_mode=pl.Buffered(1)),
                  pl.BlockSpec((None, te, D), lambda b, s, e: (layer, e, 0)),
                  pl.BlockSpec((None, D, te), lambda b, s, e: (layer, 0, e)),
                  par, par],
        out_specs=pl.BlockSpec((1, tm, D), lambda b, s, e: (b, s, 0)),
        out_shape=jax.ShapeDtypeStruct((B, S, D), F32),
        scratch_shapes=[pltpu.VMEM((tm, D), BF16),
                        pltpu.VMEM((PEER_HEADS, E // te, te // PEER_NKEYS, tm), jnp.uint32),
                        pltpu.VMEM((PEER_HEADS, E // te, te // PEER_NKEYS, tm), jnp.uint32),
                        pltpu.VMEM((PEER_HEADS, PEER_NKEYS, tm), BF16),
                        pltpu.VMEM((PEER_HEADS, PEER_NKEYS, tm), BF16),
                        pltpu.VMEM((_PEER_CAND_ROWS, tm), F32),
                        pltpu.VMEM((te // sub, sub, tm), F32),
                        pltpu.VMEM((sub, tm), BF16),
                        pltpu.VMEM((te, tm), BF16),
                        pltpu.VMEM((D, tm), F32)],
        compiler_params=_params("parallel", "parallel", "arbitrary"),
        name="peer",
    )(x, sc, sh, gate, wq_t, keys, u, v_t, lg.reshape(1, D), lb.reshape(1, D))


def _compress_kernel(pc_ref, pe_ref, w1_ref, b1_ref, w2_ref, w2t_ref, o_ref, ot_ref):
    pc = pc_ref[0, 0, 0]
    half = pc.shape[1]
    lo = (pc + pe_ref[0, 0:1, :]).astype(BF16)
    hi = (pc + pe_ref[0, 1:2, :]).astype(BF16)
    a = jnp.dot(lo, w1_ref[0, :half, :], preferred_element_type=F32)
    b = jnp.dot(hi, w1_ref[0, half:, :], preferred_element_type=F32)
    b_next = pltpu.roll(b, pc.shape[0] - 1, 0)
    hid = jax.nn.gelu(a + b_next + b1_ref[0]).astype(BF16)
    o_ref[0, 0, 0] = jnp.dot(hid, w2_ref[0], preferred_element_type=F32).astype(o_ref.dtype)
    ot_ref[0, 0, 0] = lax.dot_general(w2t_ref[0], hid, _NT, preferred_element_type=F32).astype(ot_ref.dtype)


def _compress(pieces, pe2, w1, b1, w2, w2t):
    _, B, G, NP, F = pieces.shape
    Hd = w1.shape[2]
    hd = w2.shape[2]
    return pl.pallas_call(
        _compress_kernel,
        grid=(2, B, G),
        in_specs=[pl.BlockSpec((1, 1, 1, NP, F), lambda c, b, g: (c, b, g, 0, 0)),
                  pl.BlockSpec((1, 2, F), lambda c, b, g: (c, 0, 0)),
                  pl.BlockSpec((1, 2 * F, Hd), lambda c, b, g: (c, 0, 0)),
                  pl.BlockSpec((1, 1, Hd), lambda c, b, g: (c, 0, 0)),
                  pl.BlockSpec((1, Hd, hd), lambda c, b, g: (c, 0, 0)),
                  pl.BlockSpec((1, hd, Hd), lambda c, b, g: (c, 0, 0))],
        out_specs=[pl.BlockSpec((1, 1, 1, NP, hd), lambda c, b, g: (c, b, g, 0, 0)),
                   pl.BlockSpec((1, 1, 1, hd, NP), lambda c, b, g: (c, b, g, 0, 0))],
        out_shape=[jax.ShapeDtypeStruct((2, B, G, NP, hd), BF16),
                   jax.ShapeDtypeStruct((2, B, G, hd, NP), BF16)],
        compiler_params=_params("parallel", "parallel", "parallel"),
        name="compress",
    )(pieces, pe2, w1, b1, w2, w2t)


def _softmax_t(s, mask):
    sm = jnp.where(mask, s, MASKED)
    m = jnp.maximum(jnp.max(sm, axis=0, keepdims=True), NEG_INF)
    e = jnp.exp2(sm - m)
    return e, 1.0 / jnp.maximum(jnp.sum(e, axis=0, keepdims=True), 1e-30)


def _nsa_kernel(qt_ref, gl_ref, kc_ref, vct_ref, ks_ref, vst_ref, kw_ref, vwt_ref, c2s_ref, o_ref,
                sel_ref, sc_ref, sw_ref, sa_ref, sb_ref, pa_ref, pb_ref, acc_ref, l_ref):
    R, hd, QB = NSA_REP, NSA_HEAD_DIM, Q_BLOCK
    g = pl.program_id(1)
    start = pl.program_id(2) * QB
    q4 = qt_ref[0] * LOG2E
    q_t = jnp.concatenate([q4[r * hd:(r + 1) * hd] for r in range(R)], axis=1).astype(BF16)
    t_q = start + lax.broadcasted_iota(jnp.int32, (1, QB), 1)
    lanes = lambda r: slice(r * QB, (r + 1) * QB)
    dot = lambda a, b: jnp.dot(a, b, preferred_element_type=F32)
    n_tiles = ks_ref.shape[2] // SLC_TILE

    def k_slc_tile(kt):
        return ks_ref[0, 0, pl.ds(pl.multiple_of(kt * SLC_TILE, SLC_TILE), SLC_TILE), :]

    base = pl.multiple_of(jnp.maximum(start - WINDOW, 0), QB)
    sc_ref[...] = dot(kc_ref[0, 0], q_t)
    sw_ref[...] = dot(kw_ref[0, 0, pl.ds(base, WIN_KEYS), :], q_t)
    sa_ref[...] = dot(k_slc_tile(0), q_t)

    n_cp = kc_ref.shape[2]
    n_idx = lax.broadcasted_iota(jnp.int32, (n_cp, 1), 0)
    cmask = (n_idx * CMP_STRIDE + (CMP_LEN - 1)) <= t_q
    v_cmp_t = vct_ref[0, 0]
    o_cmp, psum = [], None
    for r in range(R):
        e, inv = _softmax_t(sc_ref[:, lanes(r)], cmask)
        p = e * inv
        o_cmp.append(dot(v_cmp_t, p.astype(BF16)))
        psum = p if psum is None else psum + p

    kp = base + lax.broadcasted_iota(jnp.int32, (WIN_KEYS, 1), 0)
    wmask = (kp <= t_q) & (kp > t_q - WINDOW)

    def window_head(r):
        e, inv = _softmax_t(sw_ref[:, lanes(r)], wmask)
        return dot(vwt_ref[0, :, pl.ds(base, WIN_KEYS)], e.astype(BF16)) * inv

    c2s = c2s_ref[...]
    imp = None
    for part in _split3(psum):
        d = dot(c2s, part)
        imp = d if imp is None else imp + d
    n_slc = c2s.shape[0]
    j_idx = lax.broadcasted_iota(jnp.int32, (n_slc, 1), 0)
    cur = jnp.right_shift(t_q, SLC_BLOCK.bit_length() - 1)
    forced = (j_idx == 0) | (j_idx == cur) | (j_idx == cur - 1)
    avail = (j_idx * SLC_BLOCK) <= t_q
    score = jnp.where(avail, imp + jnp.where(forced, FORCE_BONUS, 0.0), -1.0)
    work = score
    sel = jnp.zeros(score.shape, F32)
    n_sel = min(SLC_TOPK, n_slc)
    o_win = []
    for it in range(n_sel):
        mx = jnp.max(work, axis=0, keepdims=True)
        first = jnp.min(jnp.where(work == mx, j_idx, n_slc), axis=0, keepdims=True)
        hit = j_idx == first
        sel = jnp.where(hit, 1.0, sel)
        work = jnp.where(hit, -jnp.inf, work)
        if (it + 1) % max(n_sel // R, 1) == 0 and len(o_win) < R:
            o_win.append(window_head(len(o_win)))
    o_win += [window_head(r) for r in range(len(o_win), R)]
    sel_ref[...] = jnp.where(score >= 0.0, sel, 0.0)

    per_tile = SLC_TILE // SLC_BLOCK
    row_tok = lax.broadcasted_iota(jnp.int32, (SLC_TILE, 1), 0)
    last = start // SLC_TILE

    def scores(kt):
        return dot(k_slc_tile(jnp.minimum(kt, n_tiles - 1)), q_t)

    def softmax_update(s_buf, p_buf, kt, causal, m, l):
        blk = pl.multiple_of(jnp.minimum(kt, n_tiles - 1) * per_tile, per_tile)
        sel_rows = sel_ref[pl.ds(blk, per_tile), :]
        mask = jnp.concatenate(
            [jnp.broadcast_to(sel_rows[a:a + 1, :], (SLC_BLOCK, QB)) for a in range(per_tile)], axis=0) > 0.5
        if causal:
            mask = mask & ((kt * SLC_TILE + row_tok) <= t_q)
        m_out, l_out, alphas = [], [], []
        for r in range(R):
            sm = jnp.where(mask, s_buf[:, lanes(r)], MASKED)
            m_new = jnp.maximum(m[r], jnp.max(sm, axis=0, keepdims=True))
            p = jnp.exp2(sm - m_new)
            alphas.append(jnp.exp2(m[r] - m_new))
            p_buf[:, lanes(r)] = p.astype(BF16)
            m_out.append(m_new)
            l_out.append(l[r] * alphas[r] + jnp.sum(p, axis=0, keepdims=True))
        return tuple(m_out), tuple(l_out), jnp.concatenate(alphas, axis=1)

    def v_tile_t(kt):
        k0 = pl.multiple_of(jnp.clip(kt, 0, n_tiles - 1) * SLC_TILE, SLC_TILE)
        return vst_ref[0, :, pl.ds(k0, SLC_TILE)]

    def pair_step(t0, carry, causal):
        m, l, alpha_prev = carry
        m, l, alpha0 = softmax_update(sa_ref, pa_ref, t0, causal, m, l)
        sb_ref[...] = scores(t0 + 1)
        acc_ref[...] = acc_ref[...] * alpha_prev + dot(v_tile_t(t0 - 1), pb_ref[...])
        m, l, alpha1 = softmax_update(sb_ref, pb_ref, t0 + 1, causal, m, l)
        sa_ref[...] = scores(t0 + 2)
        acc_ref[...] = acc_ref[...] * alpha0 + dot(v_tile_t(t0), pa_ref[...])
        return m, l, alpha1

    pb_ref[...] = jnp.zeros_like(pb_ref)
    acc_ref[...] = jnp.zeros_like(acc_ref)
    init = (tuple(jnp.full((1, QB), NEG_INF, F32) for _ in range(R)),
            tuple(jnp.zeros((1, QB), F32) for _ in range(R)),
            jnp.ones((1, R * QB), F32))
    n_pairs = last // 2
    carry = lax.fori_loop(0, n_pairs, lambda i, c: pair_step(2 * i, c, False), init)

    @pl.when(last % 2 == 1)
    def _():
        _, l, alpha = pair_step(last - 1, carry, True)
        acc_ref[...] = acc_ref[...] * alpha + dot(v_tile_t(last), pb_ref[...])
        l_ref[...] = jnp.concatenate(l, axis=1)

    @pl.when(last % 2 == 0)
    def _():
        m, l, alpha_prev = carry
        _, l, alpha = softmax_update(sa_ref, pa_ref, last, True, m, l)
        acc = acc_ref[...] * alpha_prev + dot(v_tile_t(last - 1), pb_ref[...])
        acc_ref[...] = acc * alpha + dot(v_tile_t(last), pa_ref[...])
        l_ref[...] = jnp.concatenate(l, axis=1)

    o_slc = [acc_ref[:, lanes(r)] * (1.0 / jnp.maximum(l_ref[:, lanes(r)], 1e-30)) for r in range(R)]

    outs = []
    for r in range(R):
        row = (g * R + r) * N_BRANCH
        gate = [jax.nn.sigmoid(gl_ref[0, pl.ds(row + br, 1), :]) for br in range(N_BRANCH)]
        outs.append(gate[0] * o_cmp[r] + gate[1] * o_slc[r] + gate[2] * o_win[r])
    o_ref[0] = jnp.concatenate(outs, axis=0).T.astype(o_ref.dtype)


def _nsa_attention(proj_t, shared, c2s_t):
    k_cmp, v_cmp_t, k_slc, k_win, v_t = shared
    B, _, S = proj_t.shape
    G, R, hd, QB = NSA_GROUPS, NSA_REP, NSA_HEAD_DIM, Q_BLOCK
    n_cp = k_cmp.shape[2]
    n_slc = S // SLC_BLOCK
    gate_blk = (NSA_HEADS * hd) // LANES
    whole = lambda shape: pl.BlockSpec((1, 1) + shape, lambda b, g, q: (b, g) + (0,) * len(shape))
    v_rows = lambda branch: pl.BlockSpec((1, hd, S), lambda b, g, q: (b, branch * G + g, 0))
    return pl.pallas_call(
        _nsa_kernel,
        grid=(B, G, S // QB),
        in_specs=[pl.BlockSpec((1, R * hd, QB), lambda b, g, q: (b, g, q)),
                  pl.BlockSpec((1, LANES, QB), lambda b, g, q: (b, gate_blk, q)),
                  whole((n_cp, hd)), whole((hd, n_cp)),
                  whole((S, hd)), v_rows(0),
                  whole((S, hd)), v_rows(1),
                  pl.BlockSpec((n_slc, n_cp), lambda b, g, q: (0, 0))],
        out_specs=pl.BlockSpec((1, QB, R * hd), lambda b, g, q: (b, q, g)),
        out_shape=jax.ShapeDtypeStruct((B, S, NSA_HEADS * hd), BF16),
        scratch_shapes=[pltpu.VMEM((n_slc, QB), F32),
                        pltpu.VMEM((n_cp, R * QB), F32),
                        pltpu.VMEM((WIN_KEYS, R * QB), F32),
                        pltpu.VMEM((SLC_TILE, R * QB), F32),
                        pltpu.VMEM((SLC_TILE, R * QB), F32),
                        pltpu.VMEM((SLC_TILE, R * QB), BF16),
                        pltpu.VMEM((SLC_TILE, R * QB), BF16),
                        pltpu.VMEM((hd, R * QB), F32),
                        pltpu.VMEM((1, R * QB), F32)],
        compiler_params=_params("parallel", "parallel", "arbitrary"),
        name="nsa_attention",
    )(proj_t, proj_t, k_cmp, v_cmp_t, k_slc, v_t, k_win, v_t, c2s_t)


def _cmp_to_slc_t(n_cp, n_slc):
    i = np.arange(n_cp)[None, :] * CMP_STRIDE
    j = np.arange(n_slc)[:, None] * SLC_BLOCK
    ov = np.clip(np.minimum(i + CMP_LEN, j + SLC_BLOCK) - np.maximum(i, j), 0, None) / CMP_LEN
    ov[:, n_cp - 1] = 0.0
    return jnp.asarray(ov, dtype=BF16)


def _nsa_shared_kv(x, kv_sc, kv_sh, w_kv, cmp_pe, cmp_w1, cmp_b1, cmp_w2):
    B, S, _ = x.shape
    G, hd = NSA_GROUPS, NSA_HEAD_DIM
    kv = _modmm(x, kv_sc, kv_sh, w_kv.astype(BF16), transposed=False, tn=512, tm=1024)
    kv = kv.reshape(B, S, N_BRANCH, 2, G, hd)
    NP = S // CMP_STRIDE
    pieces = kv[:, :, 0].reshape(B, NP, CMP_STRIDE, 2, G, hd).transpose(3, 0, 4, 1, 2, 5)
    pieces = pieces.reshape(2, B, G, NP, CMP_STRIDE * hd)
    comp, comp_t = _compress(pieces, cmp_pe.reshape(2, 2, CMP_STRIDE * hd), cmp_w1.astype(BF16),
                             cmp_b1[:, None, :], cmp_w2.astype(BF16), cmp_w2.transpose(0, 2, 1).astype(BF16))
    to_k = lambda t: t.transpose(0, 2, 1, 3).astype(BF16)
    w_v = w_kv.reshape(-1, N_BRANCH, 2, G * hd)[:, 1:, 1].reshape(-1, 2 * G * hd)
    v_t = _modmm(x, kv_sc, kv_sh, w_v.T.astype(BF16), transposed=True, tn=2 * G * hd, out_dtype=BF16)
    return comp[0], comp_t[1], to_k(kv[:, :, 1, 0]), to_k(kv[:, :, 2, 0]), v_t


def kernel(x, c, ada_w, ada_b, ln_g, ln_b, ret_w_in, ret_w_o, kv_ada_w, kv_ada_b, nsa_w_kv, cmp_pe, cmp_w1, cmp_b1, cmp_w2, nsa_w_in, nsa_w_o, peer_w_q, peer_keys, peer_u, peer_v):
    B, S, D = x.shape
    mods = _cmod(c, ada_w, ada_b)
    kv_mod = _cmod(c, kv_ada_w[None], kv_ada_b[None])[0]
    ret_tables = _retention_tables(S, D // RET_HEADS)
    hd = NSA_HEAD_DIM
    n_gate = NSA_HEADS * N_BRANCH
    c2s_t = _cmp_to_slc_t(S // CMP_STRIDE, S // SLC_BLOCK)
    u_all = peer_u.astype(BF16)
    v_t_all = peer_v.transpose(0, 2, 1).astype(BF16)
    shared = None
    for layer in range(DEPTH):
        sh1, sc1, g1, sh2, sc2, g2 = [m[:, None, :] for m in jnp.split(mods[layer], 6, axis=-1)]
        if layer < N_A_LAYERS:
            proj = _modmm(x, sc1, sh1, ret_w_in[layer].astype(BF16), transposed=False, tn=1024, tm=1024)
            a = _retention(proj, ret_tables)
            w_o = ret_w_o[layer]
        else:
            lb = layer - N_A_LAYERS
            w_in = nsa_w_in[lb]
            w_t = jnp.concatenate([w_in[:, :NSA_HEADS * hd].T * (hd ** -0.5), w_in[:, NSA_HEADS * hd:].T,
                                   jnp.zeros((LANES - n_gate, D), F32)], axis=0).astype(BF16)
            proj_t = _modmm(x, sc1, sh1, w_t, transposed=True, tn=384, tm=1024)
            a = _nsa_attention(proj_t, shared, c2s_t)
            w_o = nsa_w_o[lb]
        x = _oproj_ln(a, w_o.astype(BF16), x, g1, ln_g[layer, 0], ln_b[layer, 0])
        keys = peer_keys[layer].reshape(PEER_HEADS * 2, PEER_NKEYS, -1).astype(BF16)
        x = _peer(x, sc2, sh2, g2, peer_w_q[layer].T.astype(BF16), keys, u_all, v_t_all, layer,
                  ln_g[layer, 1], ln_b[layer, 1])
        if layer == N_A_LAYERS - 1:
            kv_sh, kv_sc = [m[:, None, :] for m in jnp.split(kv_mod, 2, axis=-1)]
            shared = _nsa_shared_kv(x, kv_sc, kv_sh, nsa_w_kv, cmp_pe, cmp_w1, cmp_b1, cmp_w2)
    return x
```

```python
import functools

import numpy as np
import jax
import jax.numpy as jnp
from jax import lax
from jax.experimental import pallas as pl
from jax.experimental.pallas import tpu as pltpu

DEPTH = 4
N_A_LAYERS = DEPTH // 2
ALPHA = (2.0 * DEPTH) ** 0.25
LN_EPS = 1e-5
NEG_INF = -1e30
MASKED = 2.0 * NEG_INF
LOG2E = 1.4426950408889634

RET_HEADS = 4
RET_CHUNK = 128

NSA_HEADS = 16
NSA_GROUPS = 4
NSA_REP = NSA_HEADS // NSA_GROUPS
NSA_HEAD_DIM = 64
N_BRANCH = 3
CMP_STRIDE = 16
CMP_LEN = 32
SLC_BLOCK = 64
SLC_TOPK = 16
WINDOW = 512
Q_BLOCK = 512
FORCE_BONUS = 100.0
SLC_TILE = 512
WIN_KEYS = WINDOW + Q_BLOCK

PEER_HEADS = 8
PEER_NKEYS = 128
PEER_TOPK = 16

LANES = 128
VMEM_LIMIT = 56 * 1024 * 1024

F32 = jnp.float32
BF16 = jnp.bfloat16
_NT = (((1,), (1,)), ((), ()))


def _params(*sem):
    return pltpu.CompilerParams(dimension_semantics=sem, vmem_limit_bytes=VMEM_LIMIT)


def _split3(a):
    hi = a.astype(BF16)
    r1 = a - hi.astype(F32)
    mid = r1.astype(BF16)
    lo = (r1 - mid.astype(F32)).astype(BF16)
    return hi, mid, lo


def _layer_norm(z, g, b):
    mu = jnp.mean(z, axis=-1, keepdims=True)
    zc = z - mu
    var = jnp.mean(zc * zc, axis=-1, keepdims=True)
    return zc * lax.rsqrt(var + LN_EPS) * g + b


_GELU_K1 = -2.0 * 0.7978845608028654 * LOG2E
_GELU_K3 = _GELU_K1 * 0.044715


def _gelu_tanh(x):
    return x * (1.0 / (1.0 + jnp.exp2(x * (_GELU_K3 * (x * x) + _GELU_K1))))


def _cmod_kernel(c_ref, w_ref, b_ref, o_ref):
    c = c_ref[...]
    ca = c * jax.nn.sigmoid(c)
    w = w_ref[0]
    c_hi, c_mid, c_lo = _split3(ca)
    w_hi, w_mid, w_lo = _split3(w)
    dot = lambda a, b: jnp.dot(a, b, preferred_element_type=F32)
    acc = dot(c_hi, w_hi) + (dot(c_hi, w_mid) + dot(c_mid, w_hi)) + (dot(c_hi, w_lo) + dot(c_mid, w_mid) + dot(c_lo, w_hi))
    o_ref[0] = acc + b_ref[0]


def _cmod(c, w, b):
    L, D, N = w.shape
    n_b = c.shape[0]
    B = 16
    c = jnp.zeros((B, D), F32).at[:n_b].set(c)
    tn = 512
    out = pl.pallas_call(
        _cmod_kernel,
        grid=(L, N // tn),
        in_specs=[pl.BlockSpec((B, D), lambda l, n: (0, 0)),
                  pl.BlockSpec((1, D, tn), lambda l, n: (l, 0, n)),
                  pl.BlockSpec((1, 1, tn), lambda l, n: (l, 0, n))],
        out_specs=pl.BlockSpec((1, B, tn), lambda l, n: (l, 0, n)),
        out_shape=jax.ShapeDtypeStruct((L, B, N), F32),
        compiler_params=_params("parallel", "parallel"),
        name="cmod",
    )(c, w, b.reshape(L, 1, N))
    return out[:, :n_b]


def _modmm_kernel(x_ref, sc_ref, sh_ref, w_ref, o_ref, h_ref, *, transposed):
    @pl.when(pl.program_id(2) == 0)
    def _():
        h = x_ref[0] * (1.0 + sc_ref[0]) + sh_ref[0]
        h_ref[...] = h.astype(BF16)

    if transposed:
        o = lax.dot_general(w_ref[...], h_ref[...], _NT, preferred_element_type=F32)
    else:
        o = jnp.dot(h_ref[...], w_ref[...], preferred_element_type=F32)
    o_ref[0] = o.astype(o_ref.dtype)


def _modmm(x, sc, sh, w, *, transposed, tn, out_dtype=F32, tm=512):
    B, S, D = x.shape
    N = w.shape[0] if transposed else w.shape[1]
    if transposed:
        w_spec = pl.BlockSpec((tn, D), lambda b, s, n: (n, 0))
        o_spec = pl.BlockSpec((1, tn, tm), lambda b, s, n: (b, n, s))
        o_shape = (B, N, S)
    else:
        w_spec = pl.BlockSpec((D, tn), lambda b, s, n: (0, n))
        o_spec = pl.BlockSpec((1, tm, tn), lambda b, s, n: (b, s, n))
        o_shape = (B, S, N)
    vec = pl.BlockSpec((1, 1, D), lambda b, s, n: (b, 0, 0))
    return pl.pallas_call(
        functools.partial(_modmm_kernel, transposed=transposed),
        grid=(B, S // tm, N // tn),
        in_specs=[pl.BlockSpec((1, tm, D), lambda b, s, n: (b, s, 0)), vec, vec, w_spec],
        out_specs=o_spec,
        out_shape=jax.ShapeDtypeStruct(o_shape, out_dtype),
        scratch_shapes=[pltpu.VMEM((tm, D), BF16)],
        compiler_params=_params("parallel", "parallel", "arbitrary"),
        name="modmm_t" if transposed else "modmm",
    )(x, sc, sh, w)


def _ret_kernel(q_ref, k_ref, v_ref, g_ref, cos_ref, sin_ref, decay_ref, qdec_ref, kdec_ref, cdec_ref,
                o_ref, state_ref, *, dk, cps):
    @pl.when(pl.program_id(2) == 0)
    def _():
        state_ref[...] = jnp.zeros_like(state_ref)

    C = RET_CHUNK
    half = dk // 2
    state = state_ref[...]
    for j in range(cps):
        rows = slice(j * C, (j + 1) * C)
        cos, sin = cos_ref[rows, :], sin_ref[rows, :]

        def rot(t):
            x1, x2 = t[:, :half], t[:, half:]
            return jnp.concatenate([x1 * cos - x2 * sin, x1 * sin + x2 * cos], axis=-1)

        q = rot(q_ref[0, rows, :])
        k = rot(k_ref[0, rows, :]) * (dk ** -0.5)
        vb = v_ref[0, rows, :].astype(BF16)
        qb = q.astype(BF16)
        s = lax.dot_general(qb, k.astype(BF16), _NT, preferred_element_type=F32) * decay_ref[0]
        inner = jnp.dot(s.astype(BF16), vb, preferred_element_type=F32)
        cross = jnp.dot(qb, state.astype(BF16), preferred_element_type=F32) * qdec_ref[0]
        kd_t = (k * kdec_ref[0]).T.astype(BF16)
        state = state * cdec_ref[0] + jnp.dot(kd_t, vb, preferred_element_type=F32)
        o = inner + cross
        mu = jnp.mean(o, axis=-1, keepdims=True)
        oc = o - mu
        var = jnp.mean(oc * oc, axis=-1, keepdims=True)
        on = oc * lax.rsqrt(var + LN_EPS)
        g = g_ref[0, rows, :]
        o_ref[0, rows, :] = (g * jax.nn.sigmoid(g) * on).astype(o_ref.dtype)
    state_ref[...] = state


def _retention(proj, tables, cps=8):
    B, S, n_in = proj.shape
    H, C = RET_HEADS, RET_CHUNK
    dk = n_in // (6 * H)
    dv = 2 * dk
    cos, sin, decay, qdec, kdec, cdec = tables
    T = cps * C
    return pl.pallas_call(
        functools.partial(_ret_kernel, dk=dk, cps=cps),
        grid=(B, H, S // T),
        in_specs=[
            pl.BlockSpec((1, T, dk), lambda b, h, c: (b, c, h)),
            pl.BlockSpec((1, T, dk), lambda b, h, c: (b, c, H + h)),
            pl.BlockSpec((1, T, dv), lambda b, h, c: (b, c, H + h)),
            pl.BlockSpec((1, T, dv), lambda b, h, c: (b, c, 2 * H + h)),
            pl.BlockSpec((T, dk // 2), lambda b, h, c: (c, 0)),
            pl.BlockSpec((T, dk // 2), lambda b, h, c: (c, 0)),
            pl.BlockSpec((1, C, C), lambda b, h, c: (h, 0, 0)),
            pl.BlockSpec((1, C, dv), lambda b, h, c: (h, 0, 0)),
            pl.BlockSpec((1, C, dk), lambda b, h, c: (h, 0, 0)),
            pl.BlockSpec((1, 1, dv), lambda b, h, c: (h, 0, 0)),
        ],
        out_specs=pl.BlockSpec((1, T, dv), lambda b, h, c: (b, c, h)),
        out_shape=jax.ShapeDtypeStruct((B, S, H * dv), BF16),
        scratch_shapes=[pltpu.VMEM((dk, dv), F32)],
        compiler_params=_params("parallel", "parallel", "arbitrary"),
        name="retention",
    )(proj, proj, proj, proj, cos, sin, decay, qdec, kdec, cdec)


def _retention_tables(S, dk):
    H, C = RET_HEADS, RET_CHUNK
    dv = 2 * dk
    pos = jnp.arange(S, dtype=F32)
    theta = 1.0 / (10000.0 ** jnp.linspace(0.0, 1.0, dk // 2, dtype=F32))
    ang = pos[:, None] * theta[None, :]
    log_g = jnp.log1p(-jnp.exp2(-5.0 - jnp.arange(H, dtype=F32)))
    idx = jnp.arange(C, dtype=F32)
    diff = idx[:, None] - idx[None, :]
    decay = jnp.where(diff >= 0, jnp.exp(jnp.maximum(diff, 0.0)[None] * log_g[:, None, None]), 0.0)
    q_dec = jnp.exp((idx + 1.0)[None] * log_g[:, None])
    k_dec = jnp.exp((C - 1.0 - idx)[None] * log_g[:, None])
    c_dec = jnp.exp(C * log_g)
    return (jnp.cos(ang), jnp.sin(ang), decay,
            jnp.broadcast_to(q_dec[:, :, None], (H, C, dv)),
            jnp.broadcast_to(k_dec[:, :, None], (H, C, dk)),
            jnp.broadcast_to(c_dec[:, None, None], (H, 1, dv)))


def _oproj_ln_kernel(a_ref, w_ref, x_ref, g_ref, lg_ref, lb_ref, o_ref):
    y = jnp.dot(a_ref[0], w_ref[...], preferred_element_type=F32)
    z = ALPHA * x_ref[0] + g_ref[0] * y
    o_ref[0] = _layer_norm(z, lg_ref[...], lb_ref[...])


def _oproj_ln(a, w, x, gate, lg, lb, tm=1024):
    B, S, D = x.shape
    K = a.shape[-1]
    vec = pl.BlockSpec((1, 1, D), lambda b, s: (b, 0, 0))
    par = pl.BlockSpec((1, D), lambda b, s: (0, 0))
    return pl.pallas_call(
        _oproj_ln_kernel,
        grid=(B, S // tm),
        in_specs=[pl.BlockSpec((1, tm, K), lambda b, s: (b, s, 0)),
                  pl.BlockSpec((K, D), lambda b, s: (0, 0)),
                  pl.BlockSpec((1, tm, D), lambda b, s: (b, s, 0)),
                  vec, par, par],
        out_specs=pl.BlockSpec((1, tm, D), lambda b, s: (b, s, 0)),
        out_shape=jax.ShapeDtypeStruct((B, S, D), F32),
        compiler_params=_params("parallel", "parallel"),
        name="oproj_ln",
    )(a, w, x, gate, lg.reshape(1, D), lb.reshape(1, D))


def _top_values(s, k):
    vals = []
    for _ in range(k):
        m = jnp.max(s, axis=0, keepdims=True)
        vals.append(m)
        s = jnp.where(s == m, -jnp.inf, s)
    return vals


_SUBLANES = 8


def _sorted_top16(s):
    n = 16
    v = [s[r * _SUBLANES:(r + 1) * _SUBLANES] for r in range(n)]

    def exchange(x, i, l, larger_first):
        hi, lo = jnp.maximum(x[i], x[l]), jnp.minimum(x[i], x[l])
        x[i], x[l] = (hi, lo) if larger_first else (lo, hi)

    k = 2
    while k <= n:
        j = k // 2
        while j >= 1:
            for i in range(n):
                if i ^ j > i:
                    exchange(v, i, i ^ j, (i & k) == 0)
            j //= 2
        k *= 2
    shift = _SUBLANES // 2
    while shift >= 1:
        other = [pltpu.roll(x, shift, 0) for x in v]
        v = [jnp.maximum(v[i], other[n - 1 - i]) for i in range(n)]
        j = n // 2
        while j >= 1:
            for i in range(n):
                if i ^ j > i:
                    exchange(v, i, i ^ j, True)
            j //= 2
        shift //= 2
    return [x[0:1] for x in v]


def _twice_bf16(x):
    b = pltpu.bitcast(x, jnp.uint32)
    return b | (b >> 16)


_PEER_PAIRS = [(p, q) for p in range(PEER_TOPK) for q in range(PEER_TOPK) if (p + 1) * (q + 1) <= PEER_TOPK + 1]
_PEER_CAND_ROWS = -(-len(_PEER_PAIRS) // 8) * 8


def _peer_kernel(x_ref, sc_ref, sh_ref, g_ref, wq_ref, keys_ref, u_ref, vt_ref, lg_ref, lb_ref, o_ref,
                 h_ref, w1w_ref, lw_ref, w2b_ref, r2b_ref, cand_ref, a_ref, ws_ref, p_ref, acc_ref, *, te, sub, kchunk, wb):
    e = pl.program_id(2)
    tm = h_ref.shape[0]

    @pl.when(e == 0)
    def _route():
        h = (x_ref[0] * (1.0 + sc_ref[0]) + sh_ref[0]).astype(BF16)
        h_ref[...] = h
        q_t = lax.dot_general(wq_ref[...], h, _NT, preferred_element_type=F32).astype(BF16)
        dq = keys_ref.shape[2]
        cand_ref[...] = jnp.full(cand_ref.shape, -jnp.inf, F32)
        for hd in range(PEER_HEADS):
            s1_all = jnp.dot(keys_ref[2 * hd], q_t[(2 * hd) * dq:(2 * hd + 1) * dq], preferred_element_type=F32)
            s2_all = jnp.dot(keys_ref[2 * hd + 1], q_t[(2 * hd + 1) * dq:(2 * hd + 2) * dq], preferred_element_type=F32)
            for lt in range(tm // LANES):
                ln = slice(lt * LANES, (lt + 1) * LANES)
                s1, s2 = s1_all[:, ln], s2_all[:, ln]
                a1 = _sorted_top16(s1)
                a2 = _sorted_top16(s2)
                rank2 = jnp.full(s2.shape, float(PEER_TOPK), F32)
                for q in reversed(range(PEER_TOPK)):
                    rank2 = jnp.where(s2 >= a2[q], float(q), rank2)
                for i, (p, q) in enumerate(_PEER_PAIRS):
                    cand_ref[i:i + 1, ln] = a1[p] + a2[q]
                c = _top_values(cand_ref[:, ln], PEER_TOPK + 1)
                z = jnp.ones_like(c[0])
                for kk in range(1, PEER_TOPK):
                    z = z + jnp.exp(c[kk] - c[0])
                inv_z = 1.0 / z
                tau = 0.5 * (c[PEER_TOPK - 1] + c[PEER_TOPK])
                n_ok = jnp.zeros_like(s1)
                for q in range(PEER_TOPK):
                    n_ok = jnp.where(s1 >= tau - a2[q], float(q + 1), n_ok)
                w1 = (jnp.exp(s1 - a1[0]) * inv_z).astype(BF16).astype(F32)
                by_step = lambda t: t.reshape(w1w_ref.shape[1], w1w_ref.shape[2], LANES)
                w1w_ref[hd, :, :, ln] = by_step(_twice_bf16(w1))
                lw_ref[hd, :, :, ln] = by_step(_twice_bf16(n_ok))
                w2b_ref[hd, :, ln] = jnp.exp(s2 - a2[0]).astype(BF16)
                r2b_ref[hd, :, ln] = rank2.astype(BF16)
        acc_ref[...] = jnp.zeros_like(acc_ref)

    per_sub = sub // PEER_NKEYS
    n_sub = te // sub

    def activations(sb):
        a_ref[sb] = lax.dot_general(u_ref[sb * sub:(sb + 1) * sub, :], h_ref[...], _NT,
                                    preferred_element_type=F32)

    activations(0)
    for sb in range(n_sub):
        for ii in range(per_sub):
            k = sb * per_sub + ii
            for lt in range(tm // wb):
                ln = slice(lt * wb, (lt + 1) * wb)
                wsum = None
                for hd in range(PEER_HEADS):
                    row_bf16 = lambda ref: pltpu.bitcast(
                        jnp.broadcast_to(ref[hd, e, k:k + 1, ln], (PEER_NKEYS // 2, wb)), BF16)
                    keep = r2b_ref[hd, :, ln] < row_bf16(lw_ref)
                    term = jnp.where(keep, w2b_ref[hd, :, ln], jnp.zeros((), BF16)) * row_bf16(w1w_ref)
                    wsum = term if wsum is None else wsum + term
                ws_ref[ii * PEER_NKEYS:(ii + 1) * PEER_NKEYS, ln] = wsum
        if sb + 1 < n_sub:
            activations(sb + 1)
        for ii in range(per_sub):
            rows = slice(ii * PEER_NKEYS, (ii + 1) * PEER_NKEYS)
            p_ref[pl.ds(sb * sub + ii * PEER_NKEYS, PEER_NKEYS), :] = (
                _gelu_tanh(a_ref[sb, rows, :]).astype(BF16) * ws_ref[rows, :])
        done = (sb + 1) * sub
        if done % kchunk == 0:
            cols = slice(done - kchunk, done)
            acc_ref[...] += jnp.dot(vt_ref[:, cols], p_ref[cols, :], preferred_element_type=F32)

    @pl.when(e == pl.num_programs(2) - 1)
    def _finish():
        y = acc_ref[...].T
        z = ALPHA * x_ref[0] + g_ref[0] * y
        o_ref[0] = _layer_norm(z, lg_ref[...], lb_ref[...])


def _peer(x, sc, sh, gate, wq_t, keys, u, v_t, layer, lg, lb, tm=512, te=2048, sub=512, kchunk=1024, wb=256):
    B, S, D = x.shape
    E = u.shape[1]
    nq = wq_t.shape[0]
    vec = pl.BlockSpec((1, 1, D), lambda b, s, e: (b, 0, 0))
    par = pl.BlockSpec((1, D), lambda b, s, e: (0, 0))
    return pl.pallas_call(
        functools.partial(_peer_kernel, te=te, sub=sub, kchunk=kchunk, wb=wb),
        grid=(B, S // tm, E // te),
        in_specs=[pl.BlockSpec((1, tm, D), lambda b, s, e: (b, s, 0)), vec, vec, vec,
                  pl.BlockSpec((nq, D), lambda b, s, e: (0, 0), pipeline_mode=pl.Buffered(1)),
                  pl.BlockSpec(keys.shape, lambda b, s, e: (0, 0, 0), pipeline_mode=pl.Buffered(1)),
                  pl.BlockSpec((None, te, D), lambda b, s, e: (layer, e, 0)),
                  pl.BlockSpec((None, D, te), lambda b, s, e: (layer, 0, e)),
                  par, par],
        out_specs=pl.BlockSpec((1, tm, D), lambda b, s, e: (b, s, 0)),
        out_shape=jax.ShapeDtypeStruct((B, S, D), F32),
        scratch_shapes=[pltpu.VMEM((tm, D), BF16),
                        pltpu.VMEM((PEER_HEADS, E // te, te // PEER_NKEYS, tm), jnp.uint32),
                        pltpu.VMEM((PEER_HEADS, E // te, te // PEER_NKEYS, tm), jnp.uint32),
                        pltpu.VMEM((PEER_HEADS, PEER_NKEYS, tm), BF16),
                        pltpu.VMEM((PEER_HEADS, PEER_NKEYS, tm), BF16),
                        pltpu.VMEM((_PEER_CAND_ROWS, tm), F32),
                        pltpu.VMEM((te // sub, sub, tm), F32),
                        pltpu.VMEM((sub, tm), BF16),
                        pltpu.VMEM((te, tm), BF16),
                        pltpu.VMEM((D, tm), F32)],
        compiler_params=_params("parallel", "parallel", "arbitrary"),
        name="peer",
    )(x, sc, sh, gate, wq_t, keys, u, v_t, lg.reshape(1, D), lb.reshape(1, D))


def _compress_kernel(pc_ref, pe_ref, w1_ref, b1_ref, w2_ref, w2t_ref, o_ref, ot_ref):
    pc = pc_ref[0, 0, 0]
    half = pc.shape[1]
    lo = (pc + pe_ref[0, 0:1, :]).astype(BF16)
    hi = (pc + pe_ref[0, 1:2, :]).astype(BF16)
    a = jnp.dot(lo, w1_ref[0, :half, :], preferred_element_type=F32)
    b = jnp.dot(hi, w1_ref[0, half:, :], preferred_element_type=F32)
    b_next = pltpu.roll(b, pc.shape[0] - 1, 0)
    hid = jax.nn.gelu(a + b_next + b1_ref[0]).astype(BF16)
    o_ref[0, 0, 0] = jnp.dot(hid, w2_ref[0], preferred_element_type=F32).astype(o_ref.dtype)
    ot_ref[0, 0, 0] = lax.dot_general(w2t_ref[0], hid, _NT, preferred_element_type=F32).astype(ot_ref.dtype)


def _compress(pieces, pe2, w1, b1, w2, w2t):
    _, B, G, NP, F = pieces.shape
    Hd = w1.shape[2]
    hd = w2.shape[2]
    return pl.pallas_call(
        _compress_kernel,
        grid=(2, B, G),
        in_specs=[pl.BlockSpec((1, 1, 1, NP, F), lambda c, b, g: (c, b, g, 0, 0)),
                  pl.BlockSpec((1, 2, F), lambda c, b, g: (c, 0, 0)),
                  pl.BlockSpec((1, 2 * F, Hd), lambda c, b, g: (c, 0, 0)),
                  pl.BlockSpec((1, 1, Hd), lambda c, b, g: (c, 0, 0)),
                  pl.BlockSpec((1, Hd, hd), lambda c, b, g: (c, 0, 0)),
                  pl.BlockSpec((1, hd, Hd), lambda c, b, g: (c, 0, 0))],
        out_specs=[pl.BlockSpec((1, 1, 1, NP, hd), lambda c, b, g: (c, b, g, 0, 0)),
                   pl.BlockSpec((1, 1, 1, hd, NP), lambda c, b, g: (c, b, g, 0, 0))],
        out_shape=[jax.ShapeDtypeStruct((2, B, G, NP, hd), BF16),
                   jax.ShapeDtypeStruct((2, B, G, hd, NP), BF16)],
        compiler_params=_params("parallel", "parallel", "parallel"),
        name="compress",
    )(pieces, pe2, w1, b1, w2, w2t)


def _softmax_t(s, mask):
    sm = jnp.where(mask, s, MASKED)
    m = jnp.maximum(jnp.max(sm, axis=0, keepdims=True), NEG_INF)
    e = jnp.exp2(sm - m)
    return e, 1.0 / jnp.maximum(jnp.sum(e, axis=0, keepdims=True), 1e-30)


def _nsa_kernel(qt_ref, gl_ref, kc_ref, vct_ref, ks_ref, vst_ref, kw_ref, vwt_ref, c2s_ref, o_ref,
                sel_ref, sc_ref, sw_ref, sa_ref, sb_ref, pa_ref, pb_ref, acc_ref, l_ref):
    R, hd, QB = NSA_REP, NSA_HEAD_DIM, Q_BLOCK
    g = pl.program_id(1)
    start = pl.program_id(2) * QB
    q4 = qt_ref[0] * LOG2E
    q_t = jnp.concatenate([q4[r * hd:(r + 1) * hd] for r in range(R)], axis=1).astype(BF16)
    t_q = start + lax.broadcasted_iota(jnp.int32, (1, QB), 1)
    lanes = lambda r: slice(r * QB, (r + 1) * QB)
    dot = lambda a, b: jnp.dot(a, b, preferred_element_type=F32)
    n_tiles = ks_ref.shape[2] // SLC_TILE

    def k_slc_tile(kt):
        return ks_ref[0, 0, pl.ds(pl.multiple_of(kt * SLC_TILE, SLC_TILE), SLC_TILE), :]

    base = pl.multiple_of(jnp.maximum(start - WINDOW, 0), QB)
    sc_ref[...] = dot(kc_ref[0, 0], q_t)
    sw_ref[...] = dot(kw_ref[0, 0, pl.ds(base, WIN_KEYS), :], q_t)
    sa_ref[...] = dot(k_slc_tile(0), q_t)

    n_cp = kc_ref.shape[2]
    n_idx = lax.broadcasted_iota(jnp.int32, (n_cp, 1), 0)
    cmask = (n_idx * CMP_STRIDE + (CMP_LEN - 1)) <= t_q
    v_cmp_t = vct_ref[0, 0]
    o_cmp, psum = [], None
    for r in range(R):
        e, inv = _softmax_t(sc_ref[:, lanes(r)], cmask)
        p = e * inv
        o_cmp.append(dot(v_cmp_t, p.astype(BF16)))
        psum = p if psum is None else psum + p

    kp = base + lax.broadcasted_iota(jnp.int32, (WIN_KEYS, 1), 0)
    wmask = (kp <= t_q) & (kp > t_q - WINDOW)

    def window_head(r):
        e, inv = _softmax_t(sw_ref[:, lanes(r)], wmask)
        return dot(vwt_ref[0, :, pl.ds(base, WIN_KEYS)], e.astype(BF16)) * inv

    c2s = c2s_ref[...]
    imp = None
    for part in _split3(psum):
        d = dot(c2s, part)
        imp = d if imp is None else imp + d
    n_slc = c2s.shape[0]
    j_idx = lax.broadcasted_iota(jnp.int32, (n_slc, 1), 0)
    cur = jnp.right_shift(t_q, SLC_BLOCK.bit_length() - 1)
    forced = (j_idx == 0) | (j_idx == cur) | (j_idx == cur - 1)
    avail = (j_idx * SLC_BLOCK) <= t_q
    score = jnp.where(avail, imp + jnp.where(forced, FORCE_BONUS, 0.0), -1.0)
    work = score
    sel = jnp.zeros(score.shape, F32)
    n_sel = min(SLC_TOPK, n_slc)
    o_win = []
    for it in range(n_sel):
        mx = jnp.max(work, axis=0, keepdims=True)
        first = jnp.min(jnp.where(work == mx, j_idx, n_slc), axis=0, keepdims=True)
        hit = j_idx == first
        sel = jnp.where(hit, 1.0, sel)
        work = jnp.where(hit, -jnp.inf, work)
        if (it + 1) % max(n_sel // R, 1) == 0 and len(o_win) < R:
            o_win.append(window_head(len(o_win)))
    o_win += [window_head(r) for r in range(len(o_win), R)]
    sel_ref[...] = jnp.where(score >= 0.0, sel, 0.0)

    per_tile = SLC_TILE // SLC_BLOCK
    row_tok = lax.broadcasted_iota(jnp.int32, (SLC_TILE, 1), 0)
    last = start // SLC_TILE

    def scores(kt):
        return dot(k_slc_tile(jnp.minimum(kt, n_tiles - 1)), q_t)

    def softmax_update(s_buf, p_buf, kt, causal, m, l):
        blk = pl.multiple_of(jnp.minimum(kt, n_tiles - 1) * per_tile, per_tile)
        sel_rows = sel_ref[pl.ds(blk, per_tile), :]
        mask = jnp.concatenate(
            [jnp.broadcast_to(sel_rows[a:a + 1, :], (SLC_BLOCK, QB)) for a in range(per_tile)], axis=0) > 0.5
        if causal:
            mask = mask & ((kt * SLC_TILE + row_tok) <= t_q)
        m_out, l_out, alphas = [], [], []
        for r in range(R):
            sm = jnp.where(mask, s_buf[:, lanes(r)], MASKED)
            m_new = jnp.maximum(m[r], jnp.max(sm, axis=0, keepdims=True))
            p = jnp.exp2(sm - m_new)
            alphas.append(jnp.exp2(m[r] - m_new))
            p_buf[:, lanes(r)] = p.astype(BF16)
            m_out.append(m_new)
            l_out.append(l[r] * alphas[r] + jnp.sum(p, axis=0, keepdims=True))
        return tuple(m_out), tuple(l_out), jnp.concatenate(alphas, axis=1)

    def v_tile_t(kt):
        k0 = pl.multiple_of(jnp.clip(kt, 0, n_tiles - 1) * SLC_TILE, SLC_TILE)
        return vst_ref[0, :, pl.ds(k0, SLC_TILE)]

    def pair_step(t0, carry, causal):
        m, l, alpha_prev = carry
        m, l, alpha0 = softmax_update(sa_ref, pa_ref, t0, causal, m, l)
        sb_ref[...] = scores(t0 + 1)
        acc_ref[...] = acc_ref[...] * alpha_prev + dot(v_tile_t(t0 - 1), pb_ref[...])
        m, l, alpha1 = softmax_update(sb_ref, pb_ref, t0 + 1, causal, m, l)
        sa_ref[...] = scores(t0 + 2)
        acc_ref[...] = acc_ref[...] * alpha0 + dot(v_tile_t(t0), pa_ref[...])
        return m, l, alpha1

    pb_ref[...] = jnp.zeros_like(pb_ref)
    acc_ref[...] = jnp.zeros_like(acc_ref)
    init = (tuple(jnp.full((1, QB), NEG_INF, F32) for _ in range(R)),
            tuple(jnp.zeros((1, QB), F32) for _ in range(R)),
            jnp.ones((1, R * QB), F32))
    n_pairs = last // 2
    carry = lax.fori_loop(0, n_pairs, lambda i, c: pair_step(2 * i, c, False), init)

    @pl.when(last % 2 == 1)
    def _():
        _, l, alpha = pair_step(last - 1, carry, True)
        acc_ref[...] = acc_ref[...] * alpha + dot(v_tile_t(last), pb_ref[...])
        l_ref[...] = jnp.concatenate(l, axis=1)

    @pl.when(last % 2 == 0)
    def _():
        m, l, alpha_prev = carry
        _, l, alpha = softmax_update(sa_ref, pa_ref, last, True, m, l)
        acc = acc_ref[...] * alpha_prev + dot(v_tile_t(last - 1), pb_ref[...])
        acc_ref[...] = acc * alpha + dot(v_tile_t(last), pa_ref[...])
        l_ref[...] = jnp.concatenate(l, axis=1)

    o_slc = [acc_ref[:, lanes(r)] * (1.0 / jnp.maximum(l_ref[:, lanes(r)], 1e-30)) for r in range(R)]

    outs = []
    for r in range(R):
        row = (g * R + r) * N_BRANCH
        gate = [jax.nn.sigmoid(gl_ref[0, pl.ds(row + br, 1), :]) for br in range(N_BRANCH)]
        outs.append(gate[0] * o_cmp[r] + gate[1] * o_slc[r] + gate[2] * o_win[r])
    o_ref[0] = jnp.concatenate(outs, axis=0).T.astype(o_ref.dtype)


def _nsa_attention(proj_t, shared, c2s_t):
    k_cmp, v_cmp_t, k_slc, k_win, v_t = shared
    B, _, S = proj_t.shape
    G, R, hd, QB = NSA_GROUPS, NSA_REP, NSA_HEAD_DIM, Q_BLOCK
    n_cp = k_cmp.shape[2]
    n_slc = S // SLC_BLOCK
    gate_blk = (NSA_HEADS * hd) // LANES
    whole = lambda shape: pl.BlockSpec((1, 1) + shape, lambda b, g, q: (b, g) + (0,) * len(shape))
    v_rows = lambda branch: pl.BlockSpec((1, hd, S), lambda b, g, q: (b, branch * G + g, 0))
    return pl.pallas_call(
        _nsa_kernel,
        grid=(B, G, S // QB),
        in_specs=[pl.BlockSpec((1, R * hd, QB), lambda b, g, q: (b, g, q)),
                  pl.BlockSpec((1, LANES, QB), lambda b, g, q: (b, gate_blk, q)),
                  whole((n_cp, hd)), whole((hd, n_cp)),
                  whole((S, hd)), v_rows(0),
                  whole((S, hd)), v_rows(1),
                  pl.BlockSpec((n_slc, n_cp), lambda b, g, q: (0, 0))],
        out_specs=pl.BlockSpec((1, QB, R * hd), lambda b, g, q: (b, q, g)),
        out_shape=jax.ShapeDtypeStruct((B, S, NSA_HEADS * hd), BF16),
        scratch_shapes=[pltpu.VMEM((n_slc, QB), F32),
                        pltpu.VMEM((n_cp, R * QB), F32),
                        pltpu.VMEM((WIN_KEYS, R * QB), F32),
                        pltpu.VMEM((SLC_TILE, R * QB), F32),
                        pltpu.VMEM((SLC_TILE, R * QB), F32),
                        pltpu.VMEM((SLC_TILE, R * QB), BF16),
                        pltpu.VMEM((SLC_TILE, R * QB), BF16),
                        pltpu.VMEM((hd, R * QB), F32),
                        pltpu.VMEM((1, R * QB), F32)],
        compiler_params=_params("parallel", "parallel", "arbitrary"),
        name="nsa_attention",
    )(proj_t, proj_t, k_cmp, v_cmp_t, k_slc, v_t, k_win, v_t, c2s_t)


def _cmp_to_slc_t(n_cp, n_slc):
    i = np.arange(n_cp)[None, :] * CMP_STRIDE
    j = np.arange(n_slc)[:, None] * SLC_BLOCK
    ov = np.clip(np.minimum(i + CMP_LEN, j + SLC_BLOCK) - np.maximum(i, j), 0, None) / CMP_LEN
    ov[:, n_cp - 1] = 0.0
    return jnp.asarray(ov, dtype=BF16)


def _nsa_shared_kv(x, kv_sc, kv_sh, w_kv, cmp_pe, cmp_w1, cmp_b1, cmp_w2):
    B, S, _ = x.shape
    G, hd = NSA_GROUPS, NSA_HEAD_DIM
    kv = _modmm(x, kv_sc, kv_sh, w_kv.astype(BF16), transposed=False, tn=512, tm=1024)
    kv = kv.reshape(B, S, N_BRANCH, 2, G, hd)
    NP = S // CMP_STRIDE
    pieces = kv[:, :, 0].reshape(B, NP, CMP_STRIDE, 2, G, hd).transpose(3, 0, 4, 1, 2, 5)
    pieces = pieces.reshape(2, B, G, NP, CMP_STRIDE * hd)
    comp, comp_t = _compress(pieces, cmp_pe.reshape(2, 2, CMP_STRIDE * hd), cmp_w1.astype(BF16),
                             cmp_b1[:, None, :], cmp_w2.astype(BF16), cmp_w2.transpose(0, 2, 1).astype(BF16))
    to_k = lambda t: t.transpose(0, 2, 1, 3).astype(BF16)
    w_v = w_kv.reshape(-1, N_BRANCH, 2, G * hd)[:, 1:, 1].reshape(-1, 2 * G * hd)
    v_t = _modmm(x, kv_sc, kv_sh, w_v.T.astype(BF16), transposed=True, tn=2 * G * hd, out_dtype=BF16)
    return comp[0], comp_t[1], to_k(kv[:, :, 1, 0]), to_k(kv[:, :, 2, 0]), v_t


def kernel(x, c, ada_w, ada_b, ln_g, ln_b, ret_w_in, ret_w_o, kv_ada_w, kv_ada_b, nsa_w_kv, cmp_pe, cmp_w1, cmp_b1, cmp_w2, nsa_w_in, nsa_w_o, peer_w_q, peer_keys, peer_u, peer_v):
    B, S, D = x.shape
    mods = _cmod(c, ada_w, ada_b)
    kv_mod = _cmod(c, kv_ada_w[None], kv_ada_b[None])[0]
    ret_tables = _retention_tables(S, D // RET_HEADS)
    hd = NSA_HEAD_DIM
    n_gate = NSA_HEADS * N_BRANCH
    c2s_t = _cmp_to_slc_t(S // CMP_STRIDE, S // SLC_BLOCK)
    u_all = peer_u.astype(BF16)
    v_t_all = peer_v.transpose(0, 2, 1).astype(BF16)
    shared = None
    for layer in range(DEPTH):
        sh1, sc1, g1, sh2, sc2, g2 = [m[:, None, :] for m in jnp.split(mods[layer], 6, axis=-1)]
        if layer < N_A_LAYERS:
            proj = _modmm(x, sc1, sh1, ret_w_in[layer].astype(BF16), transposed=False, tn=1024, tm=1024)
            a = _retention(proj, ret_tables)
            w_o = ret_w_o[layer]
        else:
            lb = layer - N_A_LAYERS
            w_in = nsa_w_in[lb]
            w_t = jnp.concatenate([w_in[:, :NSA_HEADS * hd].T * (hd ** -0.5), w_in[:, NSA_HEADS * hd:].T,
                                   jnp.zeros((LANES - n_gate, D), F32)], axis=0).astype(BF16)
            proj_t = _modmm(x, sc1, sh1, w_t, transposed=True, tn=384, tm=1024)
            a = _nsa_attention(proj_t, shared, c2s_t)
            w_o = nsa_w_o[lb]
        x = _oproj_ln(a, w_o.astype(BF16), x, g1, ln_g[layer, 0], ln_b[layer, 0])
        keys = peer_keys[layer].reshape(PEER_HEADS * 2, PEER_NKEYS, -1).astype(BF16)
        x = _peer(x, sc2, sh2, g2, peer_w_q[layer].T.astype(BF16), keys, u_all, v_t_all, layer,
                  ln_g[layer, 1], ln_b[layer, 1])
        if layer == N_A_LAYERS - 1:
            kv_sh, kv_sc = [m[:, None, :] for m in jnp.split(kv_mod, 2, axis=-1)]
            shared = _nsa_shared_kv(x, kv_sc, kv_sh, nsa_w_kv, cmp_pe, cmp_w1, cmp_b1, cmp_w2)
    return x
```
